```python
import math
import jax, jax.numpy as jnp
from jax import lax
import numpy as np

D_MODEL = 1024
BATCH = 8
SEQ = 4096
DEPTH = 4

CHUNK = 64
N_META = 16
Q_BLOCK = 128
NORM_EPS = 1e-6
SUBLN_EPS = 1e-5
SSD_D_INNER = D_MODEL
SSD_HEAD_DIM = 64
SSD_HEADS = SSD_D_INNER // SSD_HEAD_DIM
SSD_GROUPS = 4
SSD_HEADS_PER_GROUP = SSD_HEADS // SSD_GROUPS
SSD_STATE = 128
SSD_CONV = 4
SSD_BLOCK = 64
SSD_CONV_CH = SSD_D_INNER + 2 * SSD_GROUPS * SSD_STATE
ATTN_HEAD_DIM = 64
ATTN_HEADS = D_MODEL // (2 * ATTN_HEAD_DIM)
ATTN_QK_WIDTH = ATTN_HEADS * 2 * ATTN_HEAD_DIM
ATTN_V_WIDTH = ATTN_HEADS * 2 * ATTN_HEAD_DIM
ROT_DIM = ATTN_HEAD_DIM // 4
ROPE_THETA = 500000.0
LAMBDA_STD = 0.1
D_FF = ((8 * D_MODEL // 3 + 255) // 256) * 256
MLP_CONV = 3
IN_WIDTHS = (SSD_D_INNER, SSD_CONV_CH, SSD_HEADS, ATTN_QK_WIDTH, ATTN_QK_WIDTH, ATTN_V_WIDTH, D_MODEL, D_MODEL)
N_IN = sum(IN_WIDTHS)

kernel_name = "hybrid_ssd_diffattn_convglu_meta"


def rmsnorm(x, g, eps):
    xf = x.astype(jnp.float32)
    y = xf * lax.rsqrt(jnp.mean(xf * xf, axis=-1, keepdims=True) + eps)
    return (y * g.astype(jnp.float32)).astype(x.dtype)


def causal_dwconv(x, w, b):
    k_w, ch = w.shape
    y = lax.conv_general_dilated(x, w[:, None, :].astype(x.dtype), window_strides=(1,),
                                 padding=[(k_w - 1, 0)], dimension_numbers=("NWC", "WIO", "NWC"),
                                 feature_group_count=ch)
    return y + b.astype(x.dtype)


def rope_tables(n_pos):
    half = ROT_DIM // 2
    inv = 1.0 / (ROPE_THETA ** (jnp.arange(half, dtype=jnp.float32) * 2.0 / ROT_DIM))
    ang = jnp.arange(n_pos, dtype=jnp.float32)[:, None] * inv[None, :]
    return jnp.cos(ang), jnp.sin(ang)


def apply_partial_rope(t, cos, sin):
    half = ROT_DIM // 2
    c = cos[:, None, None, :].astype(t.dtype)
    s = sin[:, None, None, :].astype(t.dtype)
    r1, r2, rest = t[..., :half], t[..., half:ROT_DIM], t[..., ROT_DIM:]
    return jnp.concatenate([r1 * c - r2 * s, r2 * c + r1 * s, rest], axis=-1)


def ssd_chunked_scan(x, a_dt, b, c):
    bsz, n, G, R, P = x.shape
    Q = SSD_BLOCK
    nc = n // Q
    x = x.reshape(bsz, nc, Q, G, R, P)
    b = b.reshape(bsz, nc, Q, G, -1)
    c = c.reshape(bsz, nc, Q, G, -1)
    a = a_dt.reshape(bsz, nc, Q, G, R).transpose(0, 3, 4, 1, 2)
    a_cs = jnp.cumsum(a, axis=-1)
    causal = jnp.tril(jnp.ones((Q, Q), dtype=bool))
    decay_in = jnp.exp(jnp.where(causal, a_cs[..., :, None] - a_cs[..., None, :], -jnp.inf))
    cb = jnp.einsum("bclgn,bcsgn->bcgls", c, b)
    y_diag = jnp.einsum("bcgls,bgrcls,bcsgrp->bclgrp", cb, decay_in, x)
    decay_to_end = jnp.exp(a_cs[..., -1:] - a_cs)
    states = jnp.einsum("bclgn,bgrcl,bclgrp->bcgrpn", b, decay_to_end, x)
    chunk_decay = jnp.exp(a_cs[..., -1])

    def step(carry, inp):
        s_c, dec_c = inp
        return carry * dec_c[..., None, None] + s_c, carry

    init = jnp.zeros(states.shape[:1] + states.shape[2:], states.dtype)
    _, prev = lax.scan(step, init, (jnp.moveaxis(states, 1, 0), jnp.moveaxis(chunk_decay, -1, 0)))
    prev = jnp.moveaxis(prev, 0, 1)
    y_off = jnp.einsum("bclgn,bcgrpn,bgrcl->bclgrp", c, prev, jnp.exp(a_cs))
    return (y_diag + y_off).reshape(bsz, n, G, R, P)


def ssd_mixer(z, xbc, dt_raw, conv_w, conv_b, dt_bias, a_log, d_skip, norm_g):
    bsz, n, _ = z.shape
    G, R, P, N = SSD_GROUPS, SSD_HEADS_PER_GROUP, SSD_HEAD_DIM, SSD_STATE
    xbc = jax.nn.silu(causal_dwconv(xbc, conv_w, conv_b))
    xs, b_in, c_in = jnp.split(xbc, [SSD_D_INNER, SSD_D_INNER + G * N], axis=-1)
    xs = xs.reshape(bsz, n, G, R, P)
    b_in = b_in.reshape(bsz, n, G, N)
    c_in = c_in.reshape(bsz, n, G, N)
    dt = jax.nn.softplus(dt_raw.astype(jnp.float32) + dt_bias.astype(jnp.float32)).reshape(bsz, n, G, R)
    a = -jnp.exp(a_log.astype(jnp.float32)).reshape(G, R)
    y = ssd_chunked_scan(xs * dt[..., None].astype(xs.dtype), dt * a, b_in, c_in)
    y = y.astype(xs.dtype) + xs * d_skip.reshape(G, R)[..., None].astype(xs.dtype)
    y = y.reshape(bsz, n, SSD_D_INNER) * jax.nn.silu(z)
    y = rmsnorm(y.reshape(bsz, n, G, SSD_D_INNER // G), norm_g.reshape(G, SSD_D_INNER // G), NORM_EPS)
    return y.reshape(bsz, n, SSD_D_INNER)


def diff_attention_mixer(q, k, v, lam_q1, lam_k1, lam_q2, lam_k2, subln_g, lam_init, chunk_id, cos, sin):
    bsz, n, _ = q.shape
    nblk = n // Q_BLOCK
    d = ATTN_HEAD_DIM
    H = ATTN_HEADS
    q = apply_partial_rope(q.reshape(bsz, n, H, 2, d), cos, sin)
    k = apply_partial_rope(k.reshape(bsz, n, H, 2, d), cos, sin)
    lam = (jnp.exp(jnp.sum(lam_q1.astype(jnp.float32) * lam_k1.astype(jnp.float32)))
           - jnp.exp(jnp.sum(lam_q2.astype(jnp.float32) * lam_k2.astype(jnp.float32))) + lam_init)
    qb = q.reshape(bsz, nblk, Q_BLOCK, H, 2, d).transpose(1, 0, 4, 3, 2, 5)
    kt = k.transpose(0, 3, 2, 1, 4)
    vt = v.reshape(bsz, n, H, 2 * d).transpose(0, 2, 1, 3)
    q_cid = chunk_id.reshape(nblk, Q_BLOCK)
    scale = d ** -0.5

    def one_block(args):
        q_blk, cid_blk = args
        s = jnp.einsum("bmhqd,bmhkd->bmhqk", q_blk, kt).astype(jnp.float32) * scale
        visible = chunk_id[None, :] <= cid_blk[:, None]
        p = jax.nn.softmax(jnp.where(visible, s, -jnp.inf), axis=-1)
        attn = p[:, 0] - lam * p[:, 1]
        return jnp.einsum("bhqk,bhkv->bhqv", attn.astype(vt.dtype), vt)

    o = lax.map(one_block, (qb, q_cid))
    o = rmsnorm(o, subln_g, SUBLN_EPS) * (1.0 - lam_init)
    return o.transpose(1, 0, 3, 2, 4).reshape(bsz, n, H * 2 * d)


def conv_glu_mlp(u, w_up, conv_w, conv_b, w_down):
    hid = causal_dwconv(u @ w_up, conv_w, conv_b)
    gate, val = jnp.split(hid, 2, axis=-1)
    return (jax.nn.silu(gate) * val) @ w_down


def setup_inputs(seed: int = 0) -> dict:
    key = jax.random.key(seed)
    ks = jax.random.split(key, 24)
    f32 = jnp.float32
    nrm = lambda k, shape, s: jax.random.normal(k, shape, f32) * s
    dt0 = jnp.exp(jax.random.uniform(ks[6], (DEPTH, SSD_HEADS), f32) * (math.log(0.1) - math.log(0.001)) + math.log(0.001))
    return {
        "x": nrm(ks[0], (BATCH, SEQ, D_MODEL), 1.0),
        "meta_tokens": nrm(ks[1], (N_META, D_MODEL), 1.0),
        "norm1_g": 1.0 + nrm(ks[2], (DEPTH, D_MODEL), 0.02),
        "w_in": nrm(ks[3], (DEPTH, D_MODEL, N_IN), D_MODEL ** -0.5),
        "ssd_conv_w": nrm(ks[4], (DEPTH, SSD_CONV, SSD_CONV_CH), SSD_CONV ** -0.5),
        "ssd_conv_b": nrm(ks[5], (DEPTH, SSD_CONV_CH), 0.01),
        "ssd_dt_bias": dt0 + jnp.log(-jnp.expm1(-dt0)),
        "ssd_a_log": jnp.log(jax.random.uniform(ks[7], (DEPTH, SSD_HEADS), f32, 1.0, 16.0)),
        "ssd_d": 1.0 + nrm(ks[8], (DEPTH, SSD_HEADS), 0.02),
        "ssd_norm_g": 1.0 + nrm(ks[9], (DEPTH, SSD_D_INNER), 0.02),
        "lambda_q1": nrm(ks[10], (DEPTH, ATTN_HEAD_DIM), LAMBDA_STD),
        "lambda_k1": nrm(ks[11], (DEPTH, ATTN_HEAD_DIM), LAMBDA_STD),
        "lambda_q2": nrm(ks[12], (DEPTH, ATTN_HEAD_DIM), LAMBDA_STD),
        "lambda_k2": nrm(ks[13], (DEPTH, ATTN_HEAD_DIM), LAMBDA_STD),
        "attn_subln_g": 1.0 + nrm(ks[14], (DEPTH, 2 * ATTN_HEAD_DIM), 0.02),
        "w_ssd_branch": nrm(ks[15], (DEPTH, SSD_D_INNER, D_MODEL), SSD_D_INNER ** -0.5),
        "w_attn_branch": nrm(ks[16], (DEPTH, ATTN_V_WIDTH, D_MODEL), ATTN_V_WIDTH ** -0.5),
        "w_out": nrm(ks[17], (DEPTH, D_MODEL, D_MODEL), D_MODEL ** -0.5),
        "norm2_g": 1.0 + nrm(ks[18], (DEPTH, D_MODEL), 0.02),
        "w_up": nrm(ks[19], (DEPTH, D_MODEL, 2 * D_FF), D_MODEL ** -0.5),
        "mlp_conv_w": nrm(ks[20], (DEPTH, MLP_CONV, 2 * D_FF), MLP_CONV ** -0.5),
        "mlp_conv_b": nrm(ks[21], (DEPTH, 2 * D_FF), 0.01),
        "w_down": nrm(ks[22], (DEPTH, D_FF, D_MODEL), D_FF ** -0.5),
        "final_norm_g": 1.0 + nrm(ks[23], (D_MODEL,), 0.02),
    }


def reference(x, meta_tokens, norm1_g, w_in, ssd_conv_w, ssd_conv_b, ssd_dt_bias, ssd_a_log, ssd_d,
              ssd_norm_g, lambda_q1, lambda_k1, lambda_q2, lambda_k2, attn_subln_g, w_ssd_branch,
              w_attn_branch, w_out, norm2_g, w_up, mlp_conv_w, mlp_conv_b, w_down, final_norm_g):
    bsz, seq, _ = x.shape
    n_tok = N_META + seq
    n_pad = -(-n_tok // Q_BLOCK) * Q_BLOCK
    meta = jnp.broadcast_to(meta_tokens.astype(x.dtype)[None], (bsz, N_META, D_MODEL))
    h = jnp.concatenate([meta, x], axis=1)
    h = jnp.pad(h, ((0, 0), (0, n_pad - n_tok), (0, 0)))
    pos = jnp.arange(n_pad)
    chunk_id = jnp.where(pos < N_META, 0, 1 + (pos - N_META) // CHUNK)
    cos, sin = rope_tables(n_pad)
    splits = np.cumsum(IN_WIDTHS)[:-1].tolist()
    for l in range(DEPTH):
        lam_init = 0.8 - 0.6 * math.exp(-0.3 * l)
        u = rmsnorm(h, norm1_g[l], NORM_EPS)
        proj = u @ w_in[l]
        z, xbc, dt_raw, q, k, v, g_ssd, g_attn = jnp.split(proj, splits, axis=-1)
        y_ssd = ssd_mixer(z, xbc, dt_raw, ssd_conv_w[l], ssd_conv_b[l], ssd_dt_bias[l], ssd_a_log[l],
                          ssd_d[l], ssd_norm_g[l])
        y_attn = diff_attention_mixer(q, k, v, lambda_q1[l], lambda_k1[l], lambda_q2[l], lambda_k2[l],
                                      attn_subln_g[l], lam_init, chunk_id, cos, sin)
        merged = (jax.nn.sigmoid(g_ssd) * (y_ssd @ w_ssd_branch[l])
                  + jax.nn.sigmoid(g_attn) * (y_attn @ w_attn_branch[l]))
        h = h + merged @ w_out[l]
        h = h + conv_glu_mlp(rmsnorm(h, norm2_g[l], NORM_EPS), w_up[l], mlp_conv_w[l], mlp_conv_b[l], w_down[l])
    return rmsnorm(h, final_norm_g, NORM_EPS)[:, N_META:N_META + seq]
```

```python
import functools
import math

import jax
import jax.numpy as jnp
from jax import lax
from jax.experimental import pallas as pl
from jax.experimental.pallas import tpu as pltpu

F32 = jnp.float32
BF16 = jnp.bfloat16

D_MODEL = 1024
N_META = 16
Q_BLOCK = 128
NORM_EPS = 1e-6
SUBLN_EPS = 1e-5
SSD_HEADS = 16
SSD_HEAD_DIM = 64
SSD_GROUPS = 4
SSD_STATE = 128
SSD_CONV = 4
ATTN_HEADS = 8
ATTN_HEAD_DIM = 64
ROT_DIM = 16
ROPE_THETA = 500000.0
D_FF = 2816
MLP_CONV = 3

COL_Z, COL_XS, COL_BC, COL_Q, COL_K, COL_V, COL_GS, COL_GA = range(8)
N_PROJ = 8 * D_MODEL

SSD_T = 128
ATT_T = 256
FF_T = 256
VMEM_LIMIT = 56 * 1024 * 1024
NEG_BIG = -1e30
LOG2E = 1.4426950408889634


def _cparams(n_axes):
    return pltpu.CompilerParams(dimension_semantics=("arbitrary",) * n_axes,
                                vmem_limit_bytes=VMEM_LIMIT)


def _pick(n, candidates):
    for c in candidates:
        if n % c == 0:
            return c
    raise ValueError(f"no block size for {n}")


def _sigmoid(x):
    return 1.0 / (1.0 + jnp.exp(-x))


def _silu(x):
    return x * _sigmoid(x)


def _inproj_kernel(x_ref, g_ref, w_ref, wdt_ref, o_ref, dt_ref, u_ref):
    @pl.when(pl.program_id(1) == 0)
    def _():
        x = x_ref[...]
        ms = jnp.mean(x * x, axis=-1, keepdims=True)
        u = (x * lax.rsqrt(ms + NORM_EPS) * g_ref[...]).astype(BF16)
        u_ref[...] = u
        dt_ref[...] = jnp.dot(u, wdt_ref[...], preferred_element_type=F32)

    o_ref[...] = jnp.dot(u_ref[...], w_ref[...], preferred_element_type=F32).astype(o_ref.dtype)


def _inproj(h2d, g, w, wdt):
    m = h2d.shape[0]
    tm = _pick(m, (1024, 512, 256, 128))
    tn = 1024
    return pl.pallas_call(
        _inproj_kernel,
        grid=(m // tm, N_PROJ // tn),
        in_specs=[
            pl.BlockSpec((tm, D_MODEL), lambda i, j: (i, 0)),
            pl.BlockSpec((1, D_MODEL), lambda i, j: (0, 0)),
            pl.BlockSpec((D_MODEL, tn), lambda i, j: (0, j)),
            pl.BlockSpec((D_MODEL, 128), lambda i, j: (0, 0)),
        ],
        out_specs=[
            pl.BlockSpec((tm, tn), lambda i, j: (i, j)),
            pl.BlockSpec((tm, 128), lambda i, j: (i, 0)),
        ],
        out_shape=[
            jax.ShapeDtypeStruct((m, N_PROJ), BF16),
            jax.ShapeDtypeStruct((m, 128), F32),
        ],
        scratch_shapes=[pltpu.VMEM((tm, D_MODEL), BF16)],
        compiler_params=_cparams(2),
        name="inproj",
    )(h2d, g, w, wdt)


def _ssd_kernel(xs_ref, bc_ref, z_ref, dt_ref, cw_ref, cb_ref, dtb_ref, alog_ref, dsk_ref, ng_ref,
                y_ref, cbuf, xact, ybuf, state):
    T = SSD_T
    W = 2 * D_MODEL

    @pl.when(pl.program_id(1) == 0)
    def _():
        cbuf[0:8, :] = jnp.zeros((8, W), F32)
        state[...] = jnp.zeros(state.shape, F32)

    cbuf[8:8 + T, 0:D_MODEL] = xs_ref[0].astype(F32)
    cbuf[8:8 + T, D_MODEL:W] = bc_ref[0].astype(F32)
    CW = 512
    for c in range(W // CW):
        cs = slice(c * CW, (c + 1) * CW)
        acc = cb_ref[:, cs] + cw_ref[3:4, cs] * cbuf[8:8 + T, cs]
        acc = acc + cw_ref[2:3, cs] * cbuf[7:7 + T, cs]
        acc = acc + cw_ref[1:2, cs] * cbuf[6:6 + T, cs]
        acc = acc + cw_ref[0:1, cs] * cbuf[5:5 + T, cs]
        xact[:, cs] = _silu(acc)
    cbuf[0:8, :] = cbuf[T:T + 8, :]

    dtv = dt_ref[0] + dtb_ref[...]
    dt = jnp.maximum(dtv, 0.0) + jnp.log1p(jnp.exp(-jnp.abs(dtv)))
    adt = dt * (-jnp.exp(alog_ref[...]))
    row = lax.broadcasted_iota(jnp.int32, (T, T), 0)
    col = lax.broadcasted_iota(jnp.int32, (T, T), 1)
    causal = row >= col
    a_cs = jnp.dot(causal.astype(F32), adt, preferred_element_type=F32,
                   precision=lax.Precision.HIGHEST)
    a_cs_t = a_cs.T
    dt_t = dt.T
    lo_lane = col < SSD_HEAD_DIM
    lo_row = row < SSD_HEAD_DIM

    nt_dims = (((1,), (1,)), ((), ()))
    for g in range(SSD_GROUPS):
        b_off = D_MODEL + SSD_STATE * g
        c_off = D_MODEL + SSD_GROUPS * SSD_STATE + SSD_STATE * g
        bg = xact[:, b_off:b_off + SSD_STATE].astype(BF16)
        cg = xact[:, c_off:c_off + SSD_STATE].astype(BF16)
        cb = lax.dot_general(cg, bg, nt_dims, preferred_element_type=F32)
        for pr in range(2):
            h0 = 4 * g + 2 * pr
            ps = slice(128 * (2 * g + pr), 128 * (2 * g + pr + 1))
            xs_pair = xact[:, ps]
            mats = []
            cols = []
            for hh in (h0, h0 + 1):
                colb = jnp.broadcast_to(a_cs[:, hh:hh + 1], (T, T))
                rowb = a_cs_t[hh:hh + 1, :]
                decay = jnp.exp(jnp.where(causal, colb - rowb, -jnp.inf))
                mats.append((cb * decay * dt_t[hh:hh + 1, :]).astype(BF16))
                cols.append(colb)
            x_lo = jnp.where(lo_lane, xs_pair, 0.0).astype(BF16)
            x_hi = jnp.where(lo_lane, 0.0, xs_pair).astype(BF16)
            y_diag = (jnp.dot(mats[0], x_lo, preferred_element_type=F32)
                      + jnp.dot(mats[1], x_hi, preferred_element_type=F32))
            st = state[ps, :]
            y_off = lax.dot_general(cg, st.astype(BF16), nt_dims, preferred_element_type=F32)
            y_off = y_off * jnp.exp(jnp.where(lo_lane, cols[0], cols[1]))
            ybuf[:, ps] = y_diag + y_off + xs_pair * dsk_ref[:, ps]

            last0 = a_cs_t[h0:h0 + 1, T - 1:T]
            last1 = a_cs_t[h0 + 1:h0 + 2, T - 1:T]
            w0 = jnp.exp(last0 - a_cs_t[h0:h0 + 1, :]) * dt_t[h0:h0 + 1, :]
            w1 = jnp.exp(last1 - a_cs_t[h0 + 1:h0 + 2, :]) * dt_t[h0 + 1:h0 + 2, :]
            wmat = jnp.where(lo_row, w0, w1)
            upd = jnp.dot((xs_pair.T * wmat).astype(BF16), bg, preferred_element_type=F32)
            cdec = jnp.where(lo_row, jnp.exp(last0), jnp.exp(last1))
            state[ps, :] = st * cdec + upd

    zz = z_ref[0].astype(F32)
    gw = D_MODEL // SSD_GROUPS
    for g in range(SSD_GROUPS):
        gs = slice(g * gw, (g + 1) * gw)
        y = ybuf[:, gs] * _silu(zz[:, gs])
        ms = jnp.mean(y * y, axis=-1, keepdims=True)
        y_ref[0, :, gs] = (y * lax.rsqrt(ms + NORM_EPS) * ng_ref[:, gs]).astype(y_ref.dtype)


def _ssd(proj3, dt3, cw, cb, dtb, alog, dsk, ng):
    bsz, n, _ = proj3.shape
    T = SSD_T
    vec = lambda w: pl.BlockSpec((1, w), lambda b, t: (0, 0))
    return pl.pallas_call(
        _ssd_kernel,
        grid=(bsz, n // T),
        in_specs=[
            pl.BlockSpec((1, T, D_MODEL), lambda b, t: (b, t, COL_XS)),
            pl.BlockSpec((1, T, D_MODEL), lambda b, t: (b, t, COL_BC)),
            pl.BlockSpec((1, T, D_MODEL), lambda b, t: (b, t, COL_Z)),
            pl.BlockSpec((1, T, 128), lambda b, t: (b, t, 0)),
            pl.BlockSpec((SSD_CONV, 2 * D_MODEL), lambda b, t: (0, 0)),
            vec(2 * D_MODEL), vec(128), vec(128), vec(D_MODEL), vec(D_MODEL),
        ],
        out_specs=pl.BlockSpec((1, T, D_MODEL), lambda b, t: (b, t, 0)),
        out_shape=jax.ShapeDtypeStruct((bsz, n, D_MODEL), BF16),
        scratch_shapes=[
            pltpu.VMEM((T + 8, 2 * D_MODEL), F32),
            pltpu.VMEM((T, 2 * D_MODEL), F32),
            pltpu.VMEM((T, D_MODEL), F32),
            pltpu.VMEM((SSD_HEADS * SSD_HEAD_DIM, SSD_STATE), F32),
        ],
        compiler_params=_cparams(2),
        name="ssd",
    )(proj3, proj3, proj3, dt3, cw, cb, dtb, alog, dsk, ng)


def _rope(t, c, s):
    lane = lax.broadcasted_iota(jnp.int32, t.shape, 1)
    first_half = (lane % ATTN_HEAD_DIM) < (ROT_DIM // 2)
    partner = jnp.where(first_half, pltpu.roll(t, 128 - ROT_DIM // 2, axis=1),
                        pltpu.roll(t, ROT_DIM // 2, axis=1))
    return t * c + partner * s


def _softmax_tile(s, vt, m, l, acc):
    mn = jnp.maximum(m, jnp.max(s, axis=0, keepdims=True))
    alpha = jnp.exp2(m - mn)
    p = jnp.exp2(s - mn)
    l = alpha * l + jnp.sum(p, axis=0, keepdims=True)
    acc = alpha * acc + jnp.dot(vt, p.astype(BF16), preferred_element_type=F32)
    return mn, l, acc


def _attn_kernel(q_ref, k_ref, v_ref, c_ref, s_ref, lp_ref, g_ref, o_ref, qz0, qz1, ks, vt, *, n):
    A = ATT_T
    nq = qz0.shape[0]
    n_full = n // A
    tail = n - n_full * A
    lane = lax.broadcasted_iota(jnp.int32, (A, 128), 1)
    lo = lane < ATTN_HEAD_DIM
    qscale = ATTN_HEAD_DIM ** -0.5 * LOG2E

    def prep(rows, r, nr):
        c = c_ref[rows, :]
        s = s_ref[rows, :]
        qf = _rope(q_ref[0, rows, :].astype(F32), c, s) * qscale
        kf = _rope(k_ref[0, rows, :].astype(F32), c, s)
        vf = v_ref[0, rows, :].astype(F32)
        if nr < A:
            pad = jnp.zeros((A - nr, 128), F32)
            qf = jnp.concatenate([qf, pad], axis=0)
            kf = jnp.concatenate([kf, pad], axis=0)
            vf = jnp.concatenate([vf, pad], axis=0)
        qz0[r] = jnp.where(lo, qf, 0.0).T.astype(BF16)
        qz1[r] = jnp.where(lo, 0.0, qf).T.astype(BF16)
        vt[r] = vf.T.astype(BF16)
        return kf.astype(BF16)

    def prep_full(r, carry):
        start = pl.multiple_of(r * A, A)
        ks[pl.ds(start, A), :] = prep(pl.ds(start, A), r, A)
        return carry

    lax.fori_loop(0, n_full, prep_full, 0)
    if tail:
        ks[n_full * A:(n_full + 1) * A, :] = prep(slice(n_full * A, n), n_full, tail)
    ks[nq * A:(nq + 1) * A, :] = jnp.zeros((A, 128), BF16)
    vt[nq] = jnp.zeros((128, A), BF16)

    lp = lp_ref[...]
    lam_init = lp[4:5, 0:1]
    lam = (jnp.exp(jnp.sum(lp[0:1] * lp[1:2], axis=1, keepdims=True))
           - jnp.exp(jnp.sum(lp[2:3] * lp[3:4], axis=1, keepdims=True)) + lam_init)
    out_gain = g_ref[...] * (1.0 - lam_init)

    kk = lax.broadcasted_iota(jnp.int32, (A, A), 0)
    qq = lax.broadcasted_iota(jnp.int32, (A, A), 1)
    diag_vis = ((kk + 48) >> 6) <= ((qq + 48) >> 6)
    kk2 = lax.broadcasted_iota(jnp.int32, (128, A), 0)
    qq2 = lax.broadcasted_iota(jnp.int32, (128, A), 1)
    corner_vis = ((kk2 + A + 48) >> 6) <= ((qq2 + 48) >> 6)

    def q_block(i, carry):
        q0 = qz0[i]
        q1 = qz1[i]
        base = pl.multiple_of(i * A, A)

        def k_step(j, st):
            kt = ks[pl.ds(pl.multiple_of(j * A, A), A), :]
            v_t = vt[j]
            s0 = jnp.dot(kt, q0, preferred_element_type=F32)
            s1 = jnp.dot(kt, q1, preferred_element_type=F32)
            return _softmax_tile(s0, v_t, *st[:3]) + _softmax_tile(s1, v_t, *st[3:])

        m_init = jnp.full((1, A), NEG_BIG, F32)
        l_init = jnp.zeros((1, A), F32)
        a_init = jnp.zeros((128, A), F32)
        st = lax.fori_loop(0, i, k_step, (m_init, l_init, a_init, m_init, l_init, a_init))

        kt = ks[pl.ds(base, A), :]
        v_t = vt[i]
        vis = diag_vis & (kk + base < n)
        s0 = jnp.where(vis, jnp.dot(kt, q0, preferred_element_type=F32), NEG_BIG)
        s1 = jnp.where(vis, jnp.dot(kt, q1, preferred_element_type=F32), NEG_BIG)
        st = _softmax_tile(s0, v_t, *st[:3]) + _softmax_tile(s1, v_t, *st[3:])

        kt = ks[pl.ds(base + A, 128), :]
        v_t = vt[i + 1][:, 0:128]
        vis = corner_vis & (kk2 + (base + A) < n)
        s0 = jnp.where(vis, jnp.dot(kt, q0, preferred_element_type=F32), NEG_BIG)
        s1 = jnp.where(vis, jnp.dot(kt, q1, preferred_element_type=F32), NEG_BIG)
        st = _softmax_tile(s0, v_t, *st[:3]) + _softmax_tile(s1, v_t, *st[3:])

        m0, l0, a0, m1, l1, a1 = st
        o = a0 * (1.0 / l0) - lam * (a1 * (1.0 / l1))
        ot = o.T
        ms = jnp.mean(ot * ot, axis=-1, keepdims=True)
        res = (ot * lax.rsqrt(ms + SUBLN_EPS) * out_gain).astype(o_ref.dtype)
        o_ref[0, pl.ds(base, 128), :] = res[0:128]

        @pl.when(base + 128 < n)
        def _():
            o_ref[0, pl.ds(base + 128, 128), :] = res[128:256]

        return carry

    lax.fori_loop(0, nq, q_block, 0)


def _attention(proj3, rope_c, rope_s, lam_params, subln_g):
    bsz, n, _ = proj3.shape
    A = ATT_T
    nq = -(-n // A)
    hb = D_MODEL // 128
    kern = functools.partial(_attn_kernel, n=n)
    return pl.pallas_call(
        kern,
        grid=(bsz, ATTN_HEADS),
        in_specs=[
            pl.BlockSpec((1, n, 128), lambda b, h: (b, 0, COL_Q * hb + h)),
            pl.BlockSpec((1, n, 128), lambda b, h: (b, 0, COL_K * hb + h)),
            pl.BlockSpec((1, n, 128), lambda b, h: (b, 0, COL_V * hb + h)),
            pl.BlockSpec((n, 128), lambda b, h: (0, 0)),
            pl.BlockSpec((n, 128), lambda b, h: (0, 0)),
            pl.BlockSpec((8, 128), lambda b, h: (0, 0)),
            pl.BlockSpec((1, 128), lambda b, h: (0, 0)),
        ],
        out_specs=pl.BlockSpec((1, n, 128), lambda b, h: (b, 0, h)),
        out_shape=jax.ShapeDtypeStruct((bsz, n, D_MODEL), BF16),
        scratch_shapes=[
            pltpu.VMEM((nq, 128, A), BF16),
            pltpu.VMEM((nq, 128, A), BF16),
            pltpu.VMEM(((nq + 1) * A, 128), BF16),
            pltpu.VMEM((nq + 1, 128, A), BF16),
        ],
        compiler_params=_cparams(2),
        name="diff_attn",
    )(proj3, proj3, proj3, rope_c, rope_s, lam_params, subln_g)


def _merge_kernel(h_ref, ys_ref, ya_ref, gs_ref, ga_ref, ws_ref, wa_ref, wo_ref, o_ref):
    a = jnp.dot(ys_ref[...], ws_ref[...], preferred_element_type=F32)
    b = jnp.dot(ya_ref[...], wa_ref[...], preferred_element_type=F32)
    m = _sigmoid(gs_ref[...].astype(F32)) * a + _sigmoid(ga_ref[...].astype(F32)) * b
    o_ref[...] = h_ref[...] + jnp.dot(m.astype(BF16), wo_ref[...], preferred_element_type=F32)


def _merge(h2d, ys2d, ya2d, proj2d, ws, wa, wo):
    m = h2d.shape[0]
    tm = _pick(m, (512, 256, 128))
    row = lambda c: pl.BlockSpec((tm, D_MODEL), lambda i: (i, c))
    wspec = pl.BlockSpec((D_MODEL, D_MODEL), lambda i: (0, 0))
    return pl.pallas_call(
        _merge_kernel,
        grid=(m // tm,),
        in_specs=[row(0), row(0), row(0), row(COL_GS), row(COL_GA), wspec, wspec, wspec],
        out_specs=row(0),
        out_shape=jax.ShapeDtypeStruct((m, D_MODEL), F32),
        compiler_params=_cparams(1),
        name="merge",
    )(h2d, ys2d, ya2d, proj2d, proj2d, ws, wa, wo)


def _mlp_kernel(h_ref, g_ref, wg_ref, wv_ref, cwg_ref, cwv_ref, cbg_ref, cbv_ref, wd_ref, o_ref,
                u_ref, hbuf, carry):
    i = pl.program_id(1)
    j = pl.program_id(2)
    tm = u_ref.shape[0]

    @pl.when(j == 0)
    def _():
        x = h_ref[0]
        ms = jnp.mean(x * x, axis=-1, keepdims=True)
        u_ref[...] = (x * lax.rsqrt(ms + NORM_EPS) * g_ref[...]).astype(BF16)
        o_ref[0] = x

    u = u_ref[...]
    convs = []
    for idx, (w_ref, cw_ref, cb_ref) in enumerate(((wg_ref, cwg_ref, cbg_ref), (wv_ref, cwv_ref, cbv_ref))):
        hid = jnp.dot(u, w_ref[...], preferred_element_type=F32)
        prev = carry[j, idx]
        hbuf[idx, 0:8, :] = jnp.where(i == 0, jnp.zeros_like(prev), prev)
        hbuf[idx, 8:8 + tm, :] = hid
        conv = cb_ref[...] + cw_ref[2:3, :] * hbuf[idx, 8:8 + tm, :]
        conv = conv + cw_ref[1:2, :] * hbuf[idx, 7:7 + tm, :]
        conv = conv + cw_ref[0:1, :] * hbuf[idx, 6:6 + tm, :]
        carry[j, idx] = hbuf[idx, tm:tm + 8, :]
        convs.append(conv)
    act = (_silu(convs[0]) * convs[1]).astype(BF16)
    o_ref[0] += jnp.dot(act, wd_ref[...], preferred_element_type=F32)


def _mlp(h3, g, w_up, cw, cb, w_down):
    bsz, n, _ = h3.shape
    tm = _pick(n, (1056, 1024, 768, 512, 384, 256, 128))
    nj = D_FF // FF_T
    return pl.pallas_call(
        _mlp_kernel,
        grid=(bsz, n // tm, nj),
        in_specs=[
            pl.BlockSpec((1, tm, D_MODEL), lambda b, i, j: (b, i, 0)),
            pl.BlockSpec((1, D_MODEL), lambda b, i, j: (0, 0)),
            pl.BlockSpec((D_MODEL, FF_T), lambda b, i, j: (0, j)),
            pl.BlockSpec((D_MODEL, FF_T), lambda b, i, j: (0, j + nj)),
            pl.BlockSpec((MLP_CONV, FF_T), lambda b, i, j: (0, j)),
            pl.BlockSpec((MLP_CONV, FF_T), lambda b, i, j: (0, j + nj)),
            pl.BlockSpec((1, FF_T), lambda b, i, j: (0, j)),
            pl.BlockSpec((1, FF_T), lambda b, i, j: (0, j + nj)),
            pl.BlockSpec((FF_T, D_MODEL), lambda b, i, j: (j, 0)),
        ],
        out_specs=pl.BlockSpec((1, tm, D_MODEL), lambda b, i, j: (b, i, 0)),
        out_shape=jax.ShapeDtypeStruct((bsz, n, D_MODEL), F32),
        scratch_shapes=[
            pltpu.VMEM((tm, D_MODEL), BF16),
            pltpu.VMEM((2, tm + 8, FF_T), F32),
            pltpu.VMEM((nj, 2, 8, FF_T), F32),
        ],
        compiler_params=_cparams(3),
        name="convglu_mlp",
    )(h3, g, w_up, w_up, cw, cw, cb, cb, w_down)


def _final_kernel(a_ref, b_ref, g_ref, o_ref):
    tm = o_ref.shape[1]

    def norm(x):
        ms = jnp.mean(x * x, axis=-1, keepdims=True)
        return x * lax.rsqrt(ms + NORM_EPS) * g_ref[...]

    o_ref[0, 0:tm - N_META, :] = norm(a_ref[0, N_META:tm, :])
    o_ref[0, tm - N_META:tm, :] = norm(b_ref[0])


def _final_norm(h3, g, seq):
    bsz = h3.shape[0]
    tm = _pick(seq, (512, 256, 128, 64, 32, 16))
    per = tm // N_META
    return pl.pallas_call(
        _final_kernel,
        grid=(bsz, seq // tm),
        in_specs=[
            pl.BlockSpec((1, tm, D_MODEL), lambda b, i: (b, i, 0)),
            pl.BlockSpec((1, N_META, D_MODEL), lambda b, i: (b, (i + 1) * per, 0)),
            pl.BlockSpec((1, D_MODEL), lambda b, i: (0, 0)),
        ],
        out_specs=pl.BlockSpec((1, tm, D_MODEL), lambda b, i: (b, i, 0)),
        out_shape=jax.ShapeDtypeStruct((bsz, seq, D_MODEL), F32),
        compiler_params=_cparams(2),
        name="final_norm",
    )(h3, h3, g)


def _rope_tables(n_pos):
    half = ROT_DIM // 2
    inv = 1.0 / (ROPE_THETA ** (jnp.arange(half, dtype=F32) * 2.0 / ROT_DIM))
    ang = jnp.arange(n_pos, dtype=F32)[:, None] * inv[None, :]
    cos, sin = jnp.cos(ang), jnp.sin(ang)
    rest = ATTN_HEAD_DIM - ROT_DIM
    c64 = jnp.concatenate([cos, cos, jnp.ones((n_pos, rest), F32)], axis=1)
    s64 = jnp.concatenate([-sin, sin, jnp.zeros((n_pos, rest), F32)], axis=1)
    return jnp.concatenate([c64, c64], axis=1), jnp.concatenate([s64, s64], axis=1)


def _pad_lanes(v, width=128):
    v = v.astype(F32).reshape(1, -1)
    return jnp.pad(v, ((0, 0), (0, width - v.shape[1])))


def kernel(x, meta_tokens, norm1_g, w_in, ssd_conv_w, ssd_conv_b, ssd_dt_bias, ssd_a_log, ssd_d, ssd_norm_g, lambda_q1, lambda_k1, lambda_q2, lambda_k2, attn_subln_g, w_ssd_branch, w_attn_branch, w_out, norm2_g, w_up, mlp_conv_w, mlp_conv_b, w_down, final_norm_g):
    bsz, seq, _ = x.shape
    depth = w_in.shape[0]
    n_tok = N_META + seq
    n = -(-n_tok // Q_BLOCK) * Q_BLOCK
    meta = jnp.broadcast_to(meta_tokens.astype(x.dtype)[None], (bsz, N_META, D_MODEL))
    h = jnp.concatenate([meta, x, jnp.zeros((bsz, n - n_tok, D_MODEL), x.dtype)], axis=1)
    rope_c, rope_s = _rope_tables(n)
    dt_lo = 3 * D_MODEL
    dt_hi = dt_lo + SSD_HEADS

    for l in range(depth):
        lam_init = 0.8 - 0.6 * math.exp(-0.3 * l)
        w_main = jnp.concatenate([w_in[l][:, :dt_lo], w_in[l][:, dt_hi:]], axis=1).astype(BF16)
        w_dt = jnp.pad(w_in[l][:, dt_lo:dt_hi], ((0, 0), (0, 128 - SSD_HEADS))).astype(BF16)
        proj, dt_raw = _inproj(h.reshape(bsz * n, D_MODEL), norm1_g[l].reshape(1, D_MODEL), w_main, w_dt)
        proj3 = proj.reshape(bsz, n, N_PROJ)

        y_ssd = _ssd(proj3, dt_raw.reshape(bsz, n, 128), ssd_conv_w[l], ssd_conv_b[l].reshape(1, -1),
                     _pad_lanes(ssd_dt_bias[l]), _pad_lanes(ssd_a_log[l]),
                     jnp.repeat(ssd_d[l], SSD_HEAD_DIM).reshape(1, D_MODEL),
                     ssd_norm_g[l].reshape(1, D_MODEL))

        lam_params = jnp.concatenate([
            _pad_lanes(lambda_q1[l]), _pad_lanes(lambda_k1[l]), _pad_lanes(lambda_q2[l]),
            _pad_lanes(lambda_k2[l]), jnp.full((1, 128), lam_init, F32), jnp.zeros((3, 128), F32)], axis=0)
        y_attn = _attention(proj3, rope_c, rope_s, lam_params, attn_subln_g[l].reshape(1, 128))

        h2d = _merge(h.reshape(bsz * n, D_MODEL), y_ssd.reshape(bsz * n, D_MODEL),
                     y_attn.reshape(bsz * n, D_MODEL), proj,
                     w_ssd_branch[l].astype(BF16), w_attn_branch[l].astype(BF16), w_out[l].astype(BF16))

        h = _mlp(h2d.reshape(bsz, n, D_MODEL), norm2_g[l].reshape(1, D_MODEL), w_up[l].astype(BF16),
                 mlp_conv_w[l], mlp_conv_b[l].reshape(1, -1), w_down[l].astype(BF16))

    return _final_norm(h, final_norm_g.reshape(1, D_MODEL), seq)
```

```python
import functools
import math

import numpy as np
import jax
import jax.numpy as jnp
from jax import lax
from jax.experimental import pallas as pl
from jax.experimental.pallas import tpu as pltpu

F32 = jnp.float32
BF16 = jnp.bfloat16

D_MODEL = 1024
N_META = 16
CHUNK = 64
Q_BLOCK = 128
NORM_EPS = 1e-6
SUBLN_EPS = 1e-5
SSD_HEADS = 16
SSD_HEAD_DIM = 64
SSD_GROUPS = 4
SSD_STATE = 128
SSD_CONV = 4
ATTN_HEADS = 8
ATTN_HEAD_DIM = 64
ROT_DIM = 16
ROPE_THETA = 500000.0
D_FF = 2816
MLP_CONV = 3

COL_Z, COL_XS, COL_BC, COL_Q, COL_K, COL_V, COL_GS, COL_GA = range(8)
N_PROJ = 8 * D_MODEL

SSD_T = 128
FF_T = 256
VMEM_LIMIT = 56 * 1024 * 1024
NEG_BIG = -1e30
LOG2E = 1.4426950408889634

ATT_T = 256
ATT_W = 2 * ATT_T
ATT_WR = ATT_W + N_META
ATT_CHUNKS = ATT_W // CHUNK
ATT_RC = ATT_WR // 3
MASK_BIG = 2.0 ** 100
TYPE_PLAIN, TYPE_EVEN, TYPE_ODD, TYPE_META = range(4)


def _cparams(n_axes):
    return pltpu.CompilerParams(dimension_semantics=("arbitrary",) * n_axes,
                                vmem_limit_bytes=VMEM_LIMIT)


def _pick(n, candidates):
    for c in candidates:
        if n % c == 0:
            return c
    raise ValueError(f"no block size for {n}")


def _sigmoid(x):
    return 1.0 / (1.0 + jnp.exp(-x))


def _silu(x):
    return x * _sigmoid(x)


def _inproj_kernel(x_ref, g_ref, w_ref, wdt_ref, o_ref, dt_ref, u_ref):
    @pl.when(pl.program_id(1) == 0)
    def _():
        x = x_ref[...]
        ms = jnp.mean(x * x, axis=-1, keepdims=True)
        u = (x * lax.rsqrt(ms + NORM_EPS) * g_ref[...]).astype(BF16)
        u_ref[...] = u
        dt_ref[...] = jnp.dot(u, wdt_ref[...], preferred_element_type=F32)

    o_ref[...] = jnp.dot(u_ref[...], w_ref[...], preferred_element_type=F32).astype(o_ref.dtype)


def _inproj(h2d, g, w, wdt):
    m = h2d.shape[0]
    tm = _pick(m, (1024, 512, 256, 128))
    tn = 1024
    return pl.pallas_call(
        _inproj_kernel,
        grid=(m // tm, N_PROJ // tn),
        in_specs=[
            pl.BlockSpec((tm, D_MODEL), lambda i, j: (i, 0)),
            pl.BlockSpec((1, D_MODEL), lambda i, j: (0, 0)),
            pl.BlockSpec((D_MODEL, tn), lambda i, j: (0, j)),
            pl.BlockSpec((D_MODEL, 128), lambda i, j: (0, 0)),
        ],
        out_specs=[
            pl.BlockSpec((tm, tn), lambda i, j: (i, j)),
            pl.BlockSpec((tm, 128), lambda i, j: (i, 0)),
        ],
        out_shape=[
            jax.ShapeDtypeStruct((m, N_PROJ), BF16),
            jax.ShapeDtypeStruct((m, 128), F32),
        ],
        scratch_shapes=[pltpu.VMEM((tm, D_MODEL), BF16)],
        compiler_params=_cparams(2),
        name="inproj",
    )(h2d, g, w, wdt)


def _ssd_kernel(xs_ref, bc_ref, z_ref, dt_ref, cw_ref, cb_ref, dtb_ref, alog_ref, dsk_ref, ng_ref,
                y_ref, cbuf, xact, ybuf, state):
    T = SSD_T
    W = 2 * D_MODEL

    @pl.when(pl.program_id(1) == 0)
    def _():
        cbuf[0:8, :] = jnp.zeros((8, W), F32)
        state[...] = jnp.zeros(state.shape, F32)

    cbuf[8:8 + T, 0:D_MODEL] = xs_ref[0].astype(F32)
    cbuf[8:8 + T, D_MODEL:W] = bc_ref[0].astype(F32)
    CW = 512
    for c in range(W // CW):
        cs = slice(c * CW, (c + 1) * CW)
        acc = cb_ref[:, cs] + cw_ref[3:4, cs] * cbuf[8:8 + T, cs]
        acc = acc + cw_ref[2:3, cs] * cbuf[7:7 + T, cs]
        acc = acc + cw_ref[1:2, cs] * cbuf[6:6 + T, cs]
        acc = acc + cw_ref[0:1, cs] * cbuf[5:5 + T, cs]
        xact[:, cs] = _silu(acc)
    cbuf[0:8, :] = cbuf[T:T + 8, :]

    dtv = dt_ref[0] + dtb_ref[...]
    dt = jnp.maximum(dtv, 0.0) + jnp.log1p(jnp.exp(-jnp.abs(dtv)))
    adt = dt * (-jnp.exp(alog_ref[...]))
    row = lax.broadcasted_iota(jnp.int32, (T, T), 0)
    col = lax.broadcasted_iota(jnp.int32, (T, T), 1)
    causal = row >= col
    a_cs = jnp.dot(causal.astype(F32), adt, preferred_element_type=F32,
                   precision=lax.Precision.HIGHEST)
    a_cs_t = a_cs.T
    dt_t = dt.T
    lo_lane = col < SSD_HEAD_DIM
    lo_row = row < SSD_HEAD_DIM

    nt_dims = (((1,), (1,)), ((), ()))
    for g in range(SSD_GROUPS):
        b_off = D_MODEL + SSD_STATE * g
        c_off = D_MODEL + SSD_GROUPS * SSD_STATE + SSD_STATE * g
        bg = xact[:, b_off:b_off + SSD_STATE].astype(BF16)
        cg = xact[:, c_off:c_off + SSD_STATE].astype(BF16)
        cb = lax.dot_general(cg, bg, nt_dims, preferred_element_type=F32)
        for pr in range(2):
            h0 = 4 * g + 2 * pr
            ps = slice(128 * (2 * g + pr), 128 * (2 * g + pr + 1))
            xs_pair = xact[:, ps]
            mats = []
            cols = []
            for hh in (h0, h0 + 1):
                colb = jnp.broadcast_to(a_cs[:, hh:hh + 1], (T, T))
                rowb = a_cs_t[hh:hh + 1, :]
                decay = jnp.exp(jnp.where(causal, colb - rowb, -jnp.inf))
                mats.append((cb * decay * dt_t[hh:hh + 1, :]).astype(BF16))
                cols.append(colb)
            x_lo = jnp.where(lo_lane, xs_pair, 0.0).astype(BF16)
            x_hi = jnp.where(lo_lane, 0.0, xs_pair).astype(BF16)
            y_diag = (jnp.dot(mats[0], x_lo, preferred_element_type=F32)
                      + jnp.dot(mats[1], x_hi, preferred_element_type=F32))
            st = state[ps, :]
            y_off = lax.dot_general(cg, st.astype(BF16), nt_dims, preferred_element_type=F32)
            y_off = y_off * jnp.exp(jnp.where(lo_lane, cols[0], cols[1]))
            ybuf[:, ps] = y_diag + y_off + xs_pair * dsk_ref[:, ps]

            last0 = a_cs_t[h0:h0 + 1, T - 1:T]
            last1 = a_cs_t[h0 + 1:h0 + 2, T - 1:T]
            w0 = jnp.exp(last0 - a_cs_t[h0:h0 + 1, :]) * dt_t[h0:h0 + 1, :]
            w1 = jnp.exp(last1 - a_cs_t[h0 + 1:h0 + 2, :]) * dt_t[h0 + 1:h0 + 2, :]
            wmat = jnp.where(lo_row, w0, w1)
            upd = jnp.dot((xs_pair.T * wmat).astype(BF16), bg, preferred_element_type=F32)
            cdec = jnp.where(lo_row, jnp.exp(last0), jnp.exp(last1))
            state[ps, :] = st * cdec + upd

    zz = z_ref[0].astype(F32)
    gw = D_MODEL // SSD_GROUPS
    for g in range(SSD_GROUPS):
        gs = slice(g * gw, (g + 1) * gw)
        y = ybuf[:, gs] * _silu(zz[:, gs])
        ms = jnp.mean(y * y, axis=-1, keepdims=True)
        y_ref[0, :, gs] = (y * lax.rsqrt(ms + NORM_EPS) * ng_ref[:, gs]).astype(y_ref.dtype)


def _ssd(proj3, dt3, cw, cb, dtb, alog, dsk, ng):
    bsz, n, _ = proj3.shape
    T = SSD_T
    vec = lambda w: pl.BlockSpec((1, w), lambda b, t: (0, 0))
    return pl.pallas_call(
        _ssd_kernel,
        grid=(bsz, n // T),
        in_specs=[
            pl.BlockSpec((1, T, D_MODEL), lambda b, t: (b, t, COL_XS)),
            pl.BlockSpec((1, T, D_MODEL), lambda b, t: (b, t, COL_BC)),
            pl.BlockSpec((1, T, D_MODEL), lambda b, t: (b, t, COL_Z)),
            pl.BlockSpec((1, T, 128), lambda b, t: (b, t, 0)),
            pl.BlockSpec((SSD_CONV, 2 * D_MODEL), lambda b, t: (0, 0)),
            vec(2 * D_MODEL), vec(128), vec(128), vec(D_MODEL), vec(D_MODEL),
        ],
        out_specs=pl.BlockSpec((1, T, D_MODEL), lambda b, t: (b, t, 0)),
        out_shape=jax.ShapeDtypeStruct((bsz, n, D_MODEL), BF16),
        scratch_shapes=[
            pltpu.VMEM((T + 8, 2 * D_MODEL), F32),
            pltpu.VMEM((T, 2 * D_MODEL), F32),
            pltpu.VMEM((T, D_MODEL), F32),
            pltpu.VMEM((SSD_HEADS * SSD_HEAD_DIM, SSD_STATE), F32),
        ],
        compiler_params=_cparams(2),
        name="ssd",
    )(proj3, proj3, proj3, dt3, cw, cb, dtb, alog, dsk, ng)


def _rope(t, c, s):
    lane = lax.broadcasted_iota(jnp.int32, t.shape, 1)
    first_half = (lane % ATTN_HEAD_DIM) < (ROT_DIM // 2)
    partner = jnp.where(first_half, pltpu.roll(t, 128 - ROT_DIM // 2, axis=1),
                        pltpu.roll(t, ROT_DIM // 2, axis=1))
    return t * c + partner * s


def _mask_features():
    r = np.arange(ATT_W)
    kf = np.zeros((ATT_WR, 128), np.float32)
    kf[r, r // CHUNK] = 1.0
    kf[ATT_W:, ATT_CHUNKS] = 1.0
    qc = np.arange(ATT_T) // CHUNK
    qf = np.zeros((4, 128, ATT_T), np.float32)
    qf[TYPE_PLAIN, ATT_CHUNKS, :] = -MASK_BIG
    for c in range(ATT_CHUNKS):
        qf[TYPE_EVEN, c, :] = np.where(c > qc, -MASK_BIG, 0.0)
        qf[TYPE_ODD, c, :] = np.where(c > qc + ATT_T // CHUNK, -MASK_BIG, 0.0)
        qf[TYPE_META, c, :] = -MASK_BIG
    return jnp.asarray(kf, BF16), jnp.asarray(qf, BF16)


def _attn_kernel(q_ref, k_ref, v_ref, c_ref, s_ref, lp_ref, g_ref, kf_ref, qf_ref, o_ref,
                 qz, ks, vt, vtm, sa, sb, pbuf, m_ref, l_ref, acc_ref, *, seq):
    A = ATT_T
    WR = ATT_WR
    n = o_ref.shape[1]
    nxb = seq // A
    nwin = vt.shape[0] // 2
    lo_row = lax.broadcasted_iota(jnp.int32, (128, A), 0) < ATTN_HEAD_DIM
    qscale = ATTN_HEAD_DIM ** -0.5 * LOG2E

    def load(ref, rows):
        return ref[0, rows, :].astype(F32)

    def store_q(j, qf):
        qt = qf.T
        qz[0, j] = jnp.where(lo_row, qt, 0.0).astype(BF16)
        qz[1, j] = jnp.where(lo_row, 0.0, qt).astype(BF16)

    mrows = slice(0, N_META)
    cm, sm = c_ref[mrows, :], s_ref[mrows, :]
    k_meta = _rope(load(k_ref, mrows), cm, sm).astype(BF16)
    q_meta = _rope(load(q_ref, mrows), cm, sm) * qscale
    store_q(nxb, jnp.concatenate([q_meta, jnp.zeros((A - N_META, 128), F32)], axis=0))
    v_meta = jnp.concatenate([jnp.zeros((128 - N_META, 128), F32), load(v_ref, mrows)], axis=0)
    vtm[...] = v_meta.T.astype(BF16)
    for w in range(nwin):
        ks[w * WR + ATT_W:(w + 1) * WR, :] = k_meta
    if nxb % 2:
        ks[(nwin - 1) * WR + A:(nwin - 1) * WR + ATT_W, :] = jnp.zeros((A, 128), BF16)
        vt[nxb] = jnp.zeros((128, A), BF16)
    o_ref[0, N_META + seq:n, :] = jnp.zeros((n - N_META - seq, 128), o_ref.dtype)

    def prep_tile(j, carry):
        rows = pl.ds(pl.multiple_of(N_META + j * A, 16), A)
        c, s = c_ref[rows, :], s_ref[rows, :]
        store_q(j, _rope(load(q_ref, rows), c, s) * qscale)
        dst = pl.multiple_of((j // 2) * WR + (j % 2) * A, 16)
        ks[pl.ds(dst, A), :] = _rope(load(k_ref, rows), c, s).astype(BF16)
        vt[j] = load(v_ref, rows).T.astype(BF16)
        return carry

    lax.fori_loop(0, nxb, prep_tile, 0)

    lp = lp_ref[...]
    lam_init = lp[4:5, 0:1]
    lam = (jnp.exp(jnp.sum(lp[0:1] * lp[1:2], axis=1, keepdims=True))
           - jnp.exp(jnp.sum(lp[2:3] * lp[3:4], axis=1, keepdims=True)) + lam_init)
    out_gain = g_ref[...] * (1.0 - lam_init)

    def step(qb, x=None, y=None):
        if x is not None:
            wx, sx, mx, last = x
            m_new = [jnp.maximum(m_ref[m], mx[m]) for m in range(2)]
            alpha = [jnp.exp2(m_ref[m] - m_new[m]) for m in range(2)]
            psum = [[], []]
        if y is not None:
            wy, ftype, sy = y
            feat = qf_ref[ftype]
            rhs = [jnp.concatenate([qz[m, qb], feat], axis=0) for m in range(2)]
            ymax = [[], []]
        for c in range(WR // ATT_RC):
            rows = slice(c * ATT_RC, (c + 1) * ATT_RC)
            if y is not None:
                krows = pl.ds(pl.multiple_of(wy * WR + c * ATT_RC, 16), ATT_RC)
                kc = jnp.concatenate([ks[krows, :], kf_ref[rows, :]], axis=1)
            for m in range(2):
                if x is not None:
                    p = jnp.exp2(sx[m, rows, :] - m_new[m])
                    psum[m].append(jnp.sum(p, axis=0, keepdims=True))
                    pbuf[m, rows, :] = p.astype(BF16)
                if y is not None:
                    s = jnp.dot(kc, rhs[m], preferred_element_type=F32)
                    sy[m, rows, :] = s
                    ymax[m].append(jnp.max(s, axis=0, keepdims=True))
        if x is not None:
            for m in range(2):
                l_ref[m] = alpha[m] * l_ref[m] + functools.reduce(jnp.add, psum[m])
                pv = (jnp.dot(vt[2 * wx], pbuf[m, 0:A, :], preferred_element_type=F32)
                      + jnp.dot(vt[2 * wx + 1], pbuf[m, A:ATT_W, :], preferred_element_type=F32))
                if last:
                    pv = pv + jnp.dot(vtm[...], pbuf[m, WR - 128:WR, :], preferred_element_type=F32)
                acc_ref[m] = alpha[m] * acc_ref[m] + pv
                m_ref[m] = m_new[m]
        if y is not None:
            return tuple(functools.reduce(jnp.maximum, ymax[m]) for m in range(2))
        return None

    def q_block(qb, carry):
        is_meta = qb == nxb
        nfull = jnp.where(is_meta, 0, qb // 2)
        type_last = jnp.where(is_meta, TYPE_META, TYPE_EVEN + qb % 2)
        ftype = lambda t: jnp.where(t == nfull, type_last, TYPE_PLAIN)
        m_ref[...] = jnp.full(m_ref.shape, NEG_BIG, F32)
        l_ref[...] = jnp.zeros(l_ref.shape, F32)
        acc_ref[...] = jnp.zeros(acc_ref.shape, F32)

        mx = step(qb, y=(0, ftype(0), sa))

        def pair(u, mx):
            w = 2 * u
            mx = step(qb, x=(w, sa, mx, False), y=(w + 1, ftype(w + 1), sb))
            return step(qb, x=(w + 1, sb, mx, False), y=(w + 2, ftype(w + 2), sa))

        npair = nfull // 2
        mx = lax.fori_loop(0, npair, pair, mx)
        w_end = 2 * npair

        @pl.when(nfull == w_end)
        def _():
            step(qb, x=(w_end, sa, mx, True))

        @pl.when(nfull != w_end)
        def _():
            mx2 = step(qb, x=(w_end, sa, mx, False), y=(w_end + 1, type_last, sb))
            step(qb, x=(w_end + 1, sb, mx2, True))

        o = acc_ref[0] * (1.0 / l_ref[0]) - lam * (acc_ref[1] * (1.0 / l_ref[1]))
        ot = o.T
        ms = jnp.mean(ot * ot, axis=-1, keepdims=True)
        res = (ot * lax.rsqrt(ms + SUBLN_EPS) * out_gain).astype(o_ref.dtype)

        @pl.when(is_meta)
        def _():
            o_ref[0, 0:N_META, :] = res[0:N_META]

        @pl.when(jnp.logical_not(is_meta))
        def _():
            o_ref[0, pl.ds(pl.multiple_of(N_META + qb * A, 16), A), :] = res

        return carry

    lax.fori_loop(0, nxb + 1, q_block, 0)


def _attention(proj3, rope_c, rope_s, lam_params, subln_g, seq):
    bsz, n, _ = proj3.shape
    A = ATT_T
    assert seq % A == 0 and n >= N_META + seq
    nxb = seq // A
    nwin = -(-nxb // 2)
    hb = D_MODEL // 128
    kfeat, qfeat = _mask_features()
    kern = functools.partial(_attn_kernel, seq=seq)
    const = lambda shape: pl.BlockSpec(shape, lambda b, h: (0,) * len(shape))
    return pl.pallas_call(
        kern,
        grid=(bsz, ATTN_HEADS),
        in_specs=[
            pl.BlockSpec((1, n, 128), lambda b, h: (b, 0, COL_Q * hb + h)),
            pl.BlockSpec((1, n, 128), lambda b, h: (b, 0, COL_K * hb + h)),
            pl.BlockSpec((1, n, 128), lambda b, h: (b, 0, COL_V * hb + h)),
            const((n, 128)), const((n, 128)), const((8, 128)), const((1, 128)),
            const((ATT_WR, 128)), const((4, 128, A)),
        ],
        out_specs=pl.BlockSpec((1, n, 128), lambda b, h: (b, 0, h)),
        out_shape=jax.ShapeDtypeStruct((bsz, n, D_MODEL), BF16),
        scratch_shapes=[
            pltpu.VMEM((2, nxb + 1, 128, A), BF16),
            pltpu.VMEM((nwin * ATT_WR, 128), BF16),
            pltpu.VMEM((2 * nwin, 128, A), BF16),
            pltpu.VMEM((128, 128), BF16),
            pltpu.VMEM((2, ATT_WR, A), F32),
            pltpu.VMEM((2, ATT_WR, A), F32),
            pltpu.VMEM((2, ATT_WR, A), BF16),
            pltpu.VMEM((2, 1, A), F32),
            pltpu.VMEM((2, 1, A), F32),
            pltpu.VMEM((2, 128, A), F32),
        ],
        compiler_params=_cparams(2),
        name="diff_attn",
    )(proj3, proj3, proj3, rope_c, rope_s, lam_params, subln_g, kfeat, qfeat)


def _merge_kernel(h_ref, ys_ref, ya_ref, gs_ref, ga_ref, ws_ref, wa_ref, wo_ref, o_ref):
    a = jnp.dot(ys_ref[...], ws_ref[...], preferred_element_type=F32)
    b = jnp.dot(ya_ref[...], wa_ref[...], preferred_element_type=F32)
    m = _sigmoid(gs_ref[...].astype(F32)) * a + _sigmoid(ga_ref[...].astype(F32)) * b
    o_ref[...] = h_ref[...] + jnp.dot(m.astype(BF16), wo_ref[...], preferred_element_type=F32)


def _merge(h2d, ys2d, ya2d, proj2d, ws, wa, wo):
    m = h2d.shape[0]
    tm = _pick(m, (512, 256, 128))
    row = lambda c: pl.BlockSpec((tm, D_MODEL), lambda i: (i, c))
    wspec = pl.BlockSpec((D_MODEL, D_MODEL), lambda i: (0, 0))
    return pl.pallas_call(
        _merge_kernel,
        grid=(m // tm,),
        in_specs=[row(0), row(0), row(0), row(COL_GS), row(COL_GA), wspec, wspec, wspec],
        out_specs=row(0),
        out_shape=jax.ShapeDtypeStruct((m, D_MODEL), F32),
        compiler_params=_cparams(1),
        name="merge",
    )(h2d, ys2d, ya2d, proj2d, proj2d, ws, wa, wo)


def _mlp_kernel(h_ref, g_ref, wg_ref, wv_ref, cwg_ref, cwv_ref, cbg_ref, cbv_ref, wd_ref, o_ref,
                u_ref, hbuf, carry):
    i = pl.program_id(1)
    j = pl.program_id(2)
    tm = u_ref.shape[0]

    @pl.when(j == 0)
    def _():
        x = h_ref[0]
        ms = jnp.mean(x * x, axis=-1, keepdims=True)
        u_ref[...] = (x * lax.rsqrt(ms + NORM_EPS) * g_ref[...]).astype(BF16)
        o_ref[0] = x

    u = u_ref[...]
    convs = []
    for idx, (w_ref, cw_ref, cb_ref) in enumerate(((wg_ref, cwg_ref, cbg_ref), (wv_ref, cwv_ref, cbv_ref))):
        hid = jnp.dot(u, w_ref[...], preferred_element_type=F32)
        prev = carry[j, idx]
        hbuf[idx, 0:8, :] = jnp.where(i == 0, jnp.zeros_like(prev), prev)
        hbuf[idx, 8:8 + tm, :] = hid
        conv = cb_ref[...] + cw_ref[2:3, :] * hbuf[idx, 8:8 + tm, :]
        conv = conv + cw_ref[1:2, :] * hbuf[idx, 7:7 + tm, :]
        conv = conv + cw_ref[0:1, :] * hbuf[idx, 6:6 + tm, :]
        carry[j, idx] = hbuf[idx, tm:tm + 8, :]
        convs.append(conv)
    act = (_silu(convs[0]) * convs[1]).astype(BF16)
    o_ref[0] += jnp.dot(act, wd_ref[...], preferred_element_type=F32)


def _mlp(h3, g, w_up, cw, cb, w_down):
    bsz, n, _ = h3.shape
    tm = _pick(n, (1056, 1024, 768, 512, 384, 256, 128))
    nj = D_FF // FF_T
    return pl.pallas_call(
        _mlp_kernel,
        grid=(bsz, n // tm, nj),
        in_specs=[
            pl.BlockSpec((1, tm, D_MODEL), lambda b, i, j: (b, i, 0)),
            pl.BlockSpec((1, D_MODEL), lambda b, i, j: (0, 0)),
            pl.BlockSpec((D_MODEL, FF_T), lambda b, i, j: (0, j)),
            pl.BlockSpec((D_MODEL, FF_T), lambda b, i, j: (0, j + nj)),
            pl.BlockSpec((MLP_CONV, FF_T), lambda b, i, j: (0, j)),
            pl.BlockSpec((MLP_CONV, FF_T), lambda b, i, j: (0, j + nj)),
            pl.BlockSpec((1, FF_T), lambda b, i, j: (0, j)),
            pl.BlockSpec((1, FF_T), lambda b, i, j: (0, j + nj)),
            pl.BlockSpec((FF_T, D_MODEL), lambda b, i, j: (j, 0)),
        ],
        out_specs=pl.BlockSpec((1, tm, D_MODEL), lambda b, i, j: (b, i, 0)),
        out_shape=jax.ShapeDtypeStruct((bsz, n, D_MODEL), F32),
        scratch_shapes=[
            pltpu.VMEM((tm, D_MODEL), BF16),
            pltpu.VMEM((2, tm + 8, FF_T), F32),
            pltpu.VMEM((nj, 2, 8, FF_T), F32),
        ],
        compiler_params=_cparams(3),
        name="convglu_mlp",
    )(h3, g, w_up, w_up, cw, cw, cb, cb, w_down)


def _final_kernel(a_ref, b_ref, g_ref, o_ref):
    tm = o_ref.shape[1]

    def norm(x):
        ms = jnp.mean(x * x, axis=-1, keepdims=True)
        return x * lax.rsqrt(ms + NORM_EPS) * g_ref[...]

    o_ref[0, 0:tm - N_META, :] = norm(a_ref[0, N_META:tm, :])
    o_ref[0, tm - N_META:tm, :] = norm(b_ref[0])


def _final_norm(h3, g, seq):
    bsz = h3.shape[0]
    tm = _pick(seq, (512, 256, 128, 64, 32))
    per = tm // N_META
    return pl.pallas_call(
        _final_kernel,
        grid=(bsz, seq // tm),
        in_specs=[
            pl.BlockSpec((1, tm, D_MODEL), lambda b, i: (b, i, 0)),
            pl.BlockSpec((1, N_META, D_MODEL), lambda b, i: (b, (i + 1) * per, 0)),
            pl.BlockSpec((1, D_MODEL), lambda b, i: (0, 0)),
        ],
        out_specs=pl.BlockSpec((1, tm, D_MODEL), lambda b, i: (b, i, 0)),
        out_shape=jax.ShapeDtypeStruct((bsz, seq, D_MODEL), F32),
        compiler_params=_cparams(2),
        name="final_norm",
    )(h3, h3, g)


def _rope_tables(n_pos):
    half = ROT_DIM // 2
    inv = 1.0 / (ROPE_THETA ** (jnp.arange(half, dtype=F32) * 2.0 / ROT_DIM))
    ang = jnp.arange(n_pos, dtype=F32)[:, None] * inv[None, :]
    cos, sin = jnp.cos(ang), jnp.sin(ang)
    rest = ATTN_HEAD_DIM - ROT_DIM
    c64 = jnp.concatenate([cos, cos, jnp.ones((n_pos, rest), F32)], axis=1)
    s64 = jnp.concatenate([-sin, sin, jnp.zeros((n_pos, rest), F32)], axis=1)
    return jnp.concatenate([c64, c64], axis=1), jnp.concatenate([s64, s64], axis=1)


def _pad_lanes(v, width=128):
    v = v.astype(F32).reshape(1, -1)
    return jnp.pad(v, ((0, 0), (0, width - v.shape[1])))


def kernel(x, meta_tokens, norm1_g, w_in, ssd_conv_w, ssd_conv_b, ssd_dt_bias, ssd_a_log, ssd_d, ssd_norm_g, lambda_q1, lambda_k1, lambda_q2, lambda_k2, attn_subln_g, w_ssd_branch, w_attn_branch, w_out, norm2_g, w_up, mlp_conv_w, mlp_conv_b, w_down, final_norm_g):
    bsz, seq, _ = x.shape
    depth = w_in.shape[0]
    n_tok = N_META + seq
    n = -(-n_tok // Q_BLOCK) * Q_BLOCK
    meta = jnp.broadcast_to(meta_tokens.astype(x.dtype)[None], (bsz, N_META, D_MODEL))
    h = jnp.concatenate([meta, x, jnp.zeros((bsz, n - n_tok, D_MODEL), x.dtype)], axis=1)
    rope_c, rope_s = _rope_tables(n)
    dt_lo = 3 * D_MODEL
    dt_hi = dt_lo + SSD_HEADS

    for l in range(depth):
        lam_init = 0.8 - 0.6 * math.exp(-0.3 * l)
        w_main = jnp.concatenate([w_in[l][:, :dt_lo], w_in[l][:, dt_hi:]], axis=1).astype(BF16)
        w_dt = jnp.pad(w_in[l][:, dt_lo:dt_hi], ((0, 0), (0, 128 - SSD_HEADS))).astype(BF16)
        proj, dt_raw = _inproj(h.reshape(bsz * n, D_MODEL), norm1_g[l].reshape(1, D_MODEL), w_main, w_dt)
        proj3 = proj.reshape(bsz, n, N_PROJ)

        y_ssd = _ssd(proj3, dt_raw.reshape(bsz, n, 128), ssd_conv_w[l], ssd_conv_b[l].reshape(1, -1),
                     _pad_lanes(ssd_dt_bias[l]), _pad_lanes(ssd_a_log[l]),
                     jnp.repeat(ssd_d[l], SSD_HEAD_DIM).reshape(1, D_MODEL),
                     ssd_norm_g[l].reshape(1, D_MODEL))

        lam_params = jnp.concatenate([
            _pad_lanes(lambda_q1[l]), _pad_lanes(lambda_k1[l]), _pad_lanes(lambda_q2[l]),
            _pad_lanes(lambda_k2[l]), jnp.full((1, 128), lam_init, F32), jnp.zeros((3, 128), F32)], axis=0)
        y_attn = _attention(proj3, rope_c, rope_s, lam_params, attn_subln_g[l].reshape(1, 128), seq)

        h2d = _merge(h.reshape(bsz * n, D_MODEL), y_ssd.reshape(bsz * n, D_MODEL),
                     y_attn.reshape(bsz * n, D_MODEL), proj,
                     w_ssd_branch[l].astype(BF16), w_attn_branch[l].astype(BF16), w_out[l].astype(BF16))

        h = _mlp(h2d.reshape(bsz, n, D_MODEL), norm2_g[l].reshape(1, D_MODEL), w_up[l].astype(BF16),
                 mlp_conv_w[l], mlp_conv_b[l].reshape(1, -1), w_down[l].astype(BF16))

    return _final_norm(h, final_norm_g.reshape(1, D_MODEL), seq)
```

```python
import functools
import math

import numpy as np
import jax
import jax.numpy as jnp
from jax import lax
from jax.experimental import pallas as pl
from jax.experimental.pallas import tpu as pltpu

F32 = jnp.float32
BF16 = jnp.bfloat16

D_MODEL = 1024
N_META = 16
CHUNK = 64
Q_BLOCK = 128
NORM_EPS = 1e-6
SUBLN_EPS = 1e-5
SSD_HEADS = 16
SSD_HEAD_DIM = 64
SSD_GROUPS = 4
SSD_STATE = 128
SSD_CONV = 4
ATTN_HEADS = 8
ATTN_HEAD_DIM = 64
ROT_DIM = 16
ROPE_THETA = 500000.0
D_FF = 2816
MLP_CONV = 3

COL_Z, COL_XS, COL_BC, COL_Q, COL_K, COL_V, COL_GS, COL_GA = range(8)
N_PROJ = 8 * D_MODEL

SSD_T = 128
SSD_HALO = 16
FF_T = 256
VMEM_LIMIT = 56 * 1024 * 1024
NEG_BIG = -1e30
LOG2E = 1.4426950408889634

ATT_T = 256
ATT_W = 2 * ATT_T
ATT_RC = 256
ATT_UNROLL = 4
MASK_BIG = 2.0 ** 100
TYPE_PLAIN, TYPE_EVEN, TYPE_ODD = 0, 1, 2


def _cparams(n_axes):
    return pltpu.CompilerParams(dimension_semantics=("arbitrary",) * n_axes,
                                vmem_limit_bytes=VMEM_LIMIT)


def _pick(n, candidates):
    for c in candidates:
        if n % c == 0:
            return c
    raise ValueError(f"no block size for {n}")


def _sigmoid(x):
    return 1.0 / (1.0 + jnp.exp(-x))


def _silu(x):
    return x * _sigmoid(x)


def _inproj_kernel(x_ref, g_ref, w_ref, wdt_ref, o_ref, dt_ref, u_ref):
    @pl.when(pl.program_id(1) == 0)
    def _():
        x = x_ref[...]
        ms = jnp.mean(x * x, axis=-1, keepdims=True)
        u = (x * lax.rsqrt(ms + NORM_EPS) * g_ref[...]).astype(BF16)
        u_ref[...] = u
        dt_ref[...] = jnp.dot(u, wdt_ref[...], preferred_element_type=F32)

    o_ref[...] = jnp.dot(u_ref[...], w_ref[...], preferred_element_type=F32).astype(o_ref.dtype)


def _inproj(h2d, g, w, wdt):
    m = h2d.shape[0]
    tm = _pick(m, (1024, 512, 256, 128))
    tn = 1024
    return pl.pallas_call(
        _inproj_kernel,
        grid=(m // tm, N_PROJ // tn),
        in_specs=[
            pl.BlockSpec((tm, D_MODEL), lambda i, j: (i, 0)),
            pl.BlockSpec((1, D_MODEL), lambda i, j: (0, 0)),
            pl.BlockSpec((D_MODEL, tn), lambda i, j: (0, j)),
            pl.BlockSpec((D_MODEL, 128), lambda i, j: (0, 0)),
        ],
        out_specs=[
            pl.BlockSpec((tm, tn), lambda i, j: (i, j)),
            pl.BlockSpec((tm, 128), lambda i, j: (i, 0)),
        ],
        out_shape=[
            jax.ShapeDtypeStruct((m, N_PROJ), BF16),
            jax.ShapeDtypeStruct((m, 128), F32),
        ],
        scratch_shapes=[pltpu.VMEM((tm, D_MODEL), BF16)],
        compiler_params=_cparams(2),
        name="inproj",
    )(h2d, g, w, wdt)


def _conv_shift_matrices():
    t = np.arange(SSD_T)
    sh = np.zeros((SSD_CONV - 1, SSD_T, SSD_HALO + SSD_T), np.float32)
    for s in range(1, SSD_CONV):
        sh[s - 1, t, SSD_HALO + t - s] = 1.0
    return jnp.asarray(sh, BF16)


def _ssd_kernel(xs_ref, bc_ref, z_ref, dt_ref, cw_ref, cb_ref, dtb_ref, alog_ref, dsk_ref, ng_ref, sh_ref,
                y_ref, halo, xact, ybuf, state):
    T = SSD_T
    W = 2 * D_MODEL

    @pl.when(pl.program_id(1) == 0)
    def _():
        halo[...] = jnp.zeros(halo.shape, BF16)
        state[...] = jnp.zeros(state.shape, F32)

    CW = 512
    for c in range(W // CW):
        cs = slice(c * CW, (c + 1) * CW)
        src, off = (xs_ref, c * CW) if c * CW < D_MODEL else (bc_ref, c * CW - D_MODEL)
        cur = src[0, :, off:off + CW]
        ext = jnp.concatenate([halo[:, cs], cur], axis=0)
        acc = cb_ref[:, cs] + cw_ref[SSD_CONV - 1:SSD_CONV, cs] * cur.astype(F32)
        for s in range(1, SSD_CONV):
            k = SSD_CONV - 1 - s
            acc = acc + cw_ref[k:k + 1, cs] * jnp.dot(sh_ref[s - 1], ext, preferred_element_type=F32)
        xact[:, cs] = _silu(acc)
        halo[:, cs] = cur[T - SSD_HALO:T, :]

    dtv = (dt_ref[0] + dtb_ref[...]).T[0:SSD_HEADS, :]
    dt_t = jnp.maximum(dtv, 0.0) + jnp.log1p(jnp.exp(-jnp.abs(dtv)))
    adt_t = dt_t * (-jnp.exp(alog_ref[...]))
    row = lax.broadcasted_iota(jnp.int32, (T, T), 0)
    col = lax.broadcasted_iota(jnp.int32, (T, T), 1)
    causal = row >= col
    a_cs_t = jnp.dot(adt_t, (row <= col).astype(F32), preferred_element_type=F32,
                     precision=lax.Precision.HIGHEST)
    a_cs = jnp.concatenate([a_cs_t, jnp.zeros((T - SSD_HEADS, T), F32)], axis=0).T
    lo_lane = col < SSD_HEAD_DIM
    lo_row = row < SSD_HEAD_DIM

    nt_dims = (((1,), (1,)), ((), ()))
    for g in range(SSD_GROUPS):
        b_off = D_MODEL + SSD_STATE * g
        c_off = D_MODEL + SSD_GROUPS * SSD_STATE + SSD_STATE * g
        bg = xact[:, b_off:b_off + SSD_STATE].astype(BF16)
        cg = xact[:, c_off:c_off + SSD_STATE].astype(BF16)
        cb = lax.dot_general(cg, bg, nt_dims, preferred_element_type=F32)
        for pr in range(2):
            h0 = 4 * g + 2 * pr
            ps = slice(128 * (2 * g + pr), 128 * (2 * g + pr + 1))
            xs_pair = xact[:, ps]
            mats = []
            cols = []
            for hh in (h0, h0 + 1):
                colb = jnp.broadcast_to(a_cs[:, hh:hh + 1], (T, T))
                rowb = a_cs_t[hh:hh + 1, :]
                decay = jnp.exp(jnp.where(causal, colb - rowb, -jnp.inf))
                mats.append((cb * decay * dt_t[hh:hh + 1, :]).astype(BF16))
                cols.append(colb)
            x_lo = jnp.where(lo_lane, xs_pair, 0.0).astype(BF16)
            x_hi = jnp.where(lo_lane, 0.0, xs_pair).astype(BF16)
            y_diag = (jnp.dot(mats[0], x_lo, preferred_element_type=F32)
                      + jnp.dot(mats[1], x_hi, preferred_element_type=F32))
            st = state[ps, :]
            y_off = lax.dot_general(cg, st.astype(BF16), nt_dims, preferred_element_type=F32)
            y_off = y_off * jnp.exp(jnp.where(lo_lane, cols[0], cols[1]))
            ybuf[:, ps] = y_diag + y_off + xs_pair * dsk_ref[:, ps]

            last0 = a_cs_t[h0:h0 + 1, T - 1:T]
            last1 = a_cs_t[h0 + 1:h0 + 2, T - 1:T]
            w0 = jnp.exp(last0 - a_cs_t[h0:h0 + 1, :]) * dt_t[h0:h0 + 1, :]
            w1 = jnp.exp(last1 - a_cs_t[h0 + 1:h0 + 2, :]) * dt_t[h0 + 1:h0 + 2, :]
            wmat = jnp.where(lo_row, w0, w1)
            upd = jnp.dot((xs_pair.T * wmat).astype(BF16), bg, preferred_element_type=F32)
            cdec = jnp.where(lo_row, jnp.exp(last0), jnp.exp(last1))
            state[ps, :] = st * cdec + upd

    zz = z_ref[0].astype(F32)
    gw = D_MODEL // SSD_GROUPS
    for g in range(SSD_GROUPS):
        gs = slice(g * gw, (g + 1) * gw)
        y = ybuf[:, gs] * _silu(zz[:, gs])
        ms = jnp.mean(y * y, axis=-1, keepdims=True)
        y_ref[0, :, gs] = (y * lax.rsqrt(ms + NORM_EPS) * ng_ref[:, gs]).astype(y_ref.dtype)


def _ssd(proj3, dt3, cw, cb, dtb, alog, dsk, ng):
    bsz, n, _ = proj3.shape
    T = SSD_T
    vec = lambda w: pl.BlockSpec((1, w), lambda b, t: (0, 0))
    shifts = _conv_shift_matrices()
    return pl.pallas_call(
        _ssd_kernel,
        grid=(bsz, n // T),
        in_specs=[
            pl.BlockSpec((1, T, D_MODEL), lambda b, t: (b, t, COL_XS)),
            pl.BlockSpec((1, T, D_MODEL), lambda b, t: (b, t, COL_BC)),
            pl.BlockSpec((1, T, D_MODEL), lambda b, t: (b, t, COL_Z)),
            pl.BlockSpec((1, T, 128), lambda b, t: (b, t, 0)),
            pl.BlockSpec((SSD_CONV, 2 * D_MODEL), lambda b, t: (0, 0)),
            vec(2 * D_MODEL), vec(128),
            pl.BlockSpec((SSD_HEADS, T), lambda b, t: (0, 0)),
            vec(D_MODEL), vec(D_MODEL),
            pl.BlockSpec(shifts.shape, lambda b, t: (0, 0, 0)),
        ],
        out_specs=pl.BlockSpec((1, T, D_MODEL), lambda b, t: (b, t, 0)),
        out_shape=jax.ShapeDtypeStruct((bsz, n, D_MODEL), BF16),
        scratch_shapes=[
            pltpu.VMEM((SSD_HALO, 2 * D_MODEL), BF16),
            pltpu.VMEM((T, 2 * D_MODEL), F32),
            pltpu.VMEM((T, D_MODEL), F32),
            pltpu.VMEM((SSD_HEADS * SSD_HEAD_DIM, SSD_STATE), F32),
        ],
        compiler_params=_cparams(2),
        name="ssd",
    )(proj3, proj3, proj3, dt3, cw, cb, dtb, alog, dsk, ng, shifts)


def _rope(t, c, s):
    lane = lax.broadcasted_iota(jnp.int32, t.shape, 1)
    first_half = (lane % ATTN_HEAD_DIM) < (ROT_DIM // 2)
    partner = jnp.where(first_half, pltpu.roll(t, 128 - ROT_DIM // 2, axis=1),
                        pltpu.roll(t, ROT_DIM // 2, axis=1))
    return t * c + partner * s


def _mask_features():
    r = np.arange(ATT_W)
    kf = np.zeros((ATT_W, 128), np.float32)
    kf[r, r // CHUNK] = 1.0
    qc = np.arange(ATT_T) // CHUNK
    qf = np.zeros((3, 128, ATT_T), np.float32)
    for c in range(ATT_W // CHUNK):
        qf[TYPE_EVEN, c, :] = np.where(c > qc, -MASK_BIG, 0.0)
        qf[TYPE_ODD, c, :] = np.where(c > qc + ATT_T // CHUNK, -MASK_BIG, 0.0)
    return jnp.asarray(kf, BF16), jnp.asarray(qf, BF16)


def _attn_schedule(nb):
    rows = []
    for b in range(nb):
        for w in range(b // 2 + 1):
            rows.append((b, w, TYPE_PLAIN if w < b // 2 else TYPE_EVEN + b % 2, 0))
    return np.asarray(rows, np.int32)


def _attn_kernel(desc_ref, q_ref, k_ref, v_ref, c_ref, s_ref, lp_ref, g_ref, kf_ref, qf_ref, o_ref,
                 qz, ks, km, vt, vtm, sa, sb, pa, pb, m_all, l_all, acc_all, *, seq, n_entries):
    A = ATT_T
    W = ATT_W
    RC = ATT_RC
    n = o_ref.shape[1]
    nb = seq // A
    lo_row = lax.broadcasted_iota(jnp.int32, (128, A), 0) < ATTN_HEAD_DIM
    qscale = ATTN_HEAD_DIM ** -0.5 * LOG2E

    def load(ref, rows):
        return ref[0, rows, :].astype(F32)

    def store_q(j, qf):
        qt = qf.T
        qz[0, j] = jnp.where(lo_row, qt, 0.0).astype(BF16)
        qz[1, j] = jnp.where(lo_row, 0.0, qt).astype(BF16)

    mrows = slice(0, N_META)
    cm, sm = c_ref[mrows, :], s_ref[mrows, :]
    km[...] = _rope(load(k_ref, mrows), cm, sm).astype(BF16)
    q_meta = _rope(load(q_ref, mrows), cm, sm) * qscale
    store_q(nb, jnp.concatenate([q_meta, jnp.zeros((A - N_META, 128), F32)], axis=0))
    v_meta = jnp.concatenate([load(v_ref, mrows), jnp.zeros((128 - N_META, 128), F32)], axis=0)
    vtm[...] = v_meta.T.astype(BF16)
    o_ref[0, N_META + seq:n, :] = jnp.zeros((n - N_META - seq, 128), o_ref.dtype)
    m_all[...] = jnp.full(m_all.shape, NEG_BIG, F32)
    l_all[...] = jnp.zeros(l_all.shape, F32)
    acc_all[...] = jnp.zeros(acc_all.shape, F32)
    if nb % 2:
        ks[nb * A:(nb + 1) * A, :] = jnp.zeros((A, 128), BF16)
        vt[nb] = jnp.zeros((128, A), BF16)

    def prep_block(j, carry):
        rows = pl.ds(pl.multiple_of(N_META + j * A, 16), A)
        c, s = c_ref[rows, :], s_ref[rows, :]
        store_q(j, _rope(load(q_ref, rows), c, s) * qscale)
        ks[pl.ds(pl.multiple_of(j * A, A), A), :] = _rope(load(k_ref, rows), c, s).astype(BF16)
        vt[j] = load(v_ref, rows).T.astype(BF16)
        return carry

    lax.fori_loop(0, nb, prep_block, 0)

    lp = lp_ref[...]
    lam_init = lp[4:5, 0:1]
    lam = (jnp.exp(jnp.sum(lp[0:1] * lp[1:2], axis=1, keepdims=True))
           - jnp.exp(jnp.sum(lp[2:3] * lp[3:4], axis=1, keepdims=True)) + lam_init)
    out_gain = g_ref[...] * (1.0 - lam_init)

    def step(t, par, mx, al):
        has_y = isinstance(t, jax.Array) or t < n_entries
        has_x = isinstance(t, jax.Array) or 1 <= t <= n_entries
        has_z = isinstance(t, jax.Array) or 2 <= t <= n_entries + 1
        s_y, s_x = (sa, sb) if par == 0 else (sb, sa)
        p_z, p_x = (pa, pb) if par == 0 else (pb, pa)
        new_mx, new_al = mx, al
        if has_z:
            qbz, wz = desc_ref[t - 2, 0], desc_ref[t - 2, 1]
            for m in range(2):
                pv = (jnp.dot(vt[2 * wz], p_z[m, 0:A, :], preferred_element_type=F32)
                      + jnp.dot(vt[2 * wz + 1], p_z[m, A:W, :], preferred_element_type=F32))
                acc_all[qbz, m] = al[m] * acc_all[qbz, m] + pv
        if has_x:
            qbx = desc_ref[t - 1, 0]
            m_old = [m_all[qbx, m] for m in range(2)]
            m_new = [jnp.maximum(m_old[m], mx[m]) for m in range(2)]
            new_al = tuple(jnp.exp2(m_old[m] - m_new[m]) for m in range(2))
            psum = [[], []]
        if has_y:
            qby, wy = desc_ref[t, 0], desc_ref[t, 1]
            feat = qf_ref[desc_ref[t, 2]]
            rhs = [jnp.concatenate([qz[m, qby], feat], axis=0) for m in range(2)]
            ymax = [[], []]
        for c in range(W // RC):
            rows = slice(c * RC, (c + 1) * RC)
            if has_y:
                krows = pl.ds(pl.multiple_of(wy * W + c * RC, 16), RC)
                kc = jnp.concatenate([ks[krows, :], kf_ref[rows, :]], axis=1)
            for m in range(2):
                if has_x:
                    p = jnp.exp2(s_x[m, rows, :] - m_new[m])
                    psum[m].append(jnp.sum(p, axis=0, keepdims=True))
                    p_x[m, rows, :] = p.astype(BF16)
                if has_y:
                    s = jnp.dot(kc, rhs[m], preferred_element_type=F32)
                    s_y[m, rows, :] = s
                    ymax[m].append(jnp.max(s, axis=0, keepdims=True))
        if has_x:
            for m in range(2):
                l_all[qbx, m] = new_al[m] * l_all[qbx, m] + functools.reduce(jnp.add, psum[m])
                m_all[qbx, m] = m_new[m]
        if has_y:
            new_mx = tuple(functools.reduce(jnp.maximum, ymax[m]) for m in range(2))
        return new_mx, new_al

    row0 = jnp.zeros((1, A), F32)
    mx, al = (row0, row0), (row0, row0)
    for t in (0, 1):
        mx, al = step(t, t % 2, mx, al)

    def group(u, carry):
        mx, al = carry[0:2], carry[2:4]
        for k in range(ATT_UNROLL):
            mx, al = step(2 + ATT_UNROLL * u + k, k % 2, mx, al)
        return mx + al

    n_mid = max(n_entries - 2, 0)
    carry = lax.fori_loop(0, n_mid // ATT_UNROLL, group, mx + al)
    mx, al = carry[0:2], carry[2:4]
    for t in range(2 + ATT_UNROLL * (n_mid // ATT_UNROLL), n_entries + 2):
        mx, al = step(t, t % 2, mx, al)

    kme = jnp.concatenate([km[...], jnp.zeros((N_META, 128), BF16)], axis=1)

    def finish(qb):
        outs = []
        for m in range(2):
            rhs = jnp.concatenate([qz[m, qb], qf_ref[TYPE_PLAIN]], axis=0)
            s = jnp.dot(kme, rhs, preferred_element_type=F32)
            m_old = m_all[qb, m]
            m_new = jnp.maximum(m_old, jnp.max(s, axis=0, keepdims=True))
            alpha = jnp.exp2(m_old - m_new)
            p = jnp.exp2(s - m_new)
            l = alpha * l_all[qb, m] + jnp.sum(p, axis=0, keepdims=True)
            pp = jnp.concatenate([p.astype(BF16), jnp.zeros((128 - N_META, A), BF16)], axis=0)
            acc = alpha * acc_all[qb, m] + jnp.dot(vtm[...], pp, preferred_element_type=F32)
            outs.append(acc * (1.0 / l))
        ot = (outs[0] - lam * outs[1]).T
        ms = jnp.mean(ot * ot, axis=-1, keepdims=True)
        res = (ot * lax.rsqrt(ms + SUBLN_EPS) * out_gain).astype(o_ref.dtype)

        if qb == nb:
            o_ref[0, 0:N_META, :] = res[0:N_META]
        else:
            o_ref[0, N_META + qb * A:N_META + (qb + 1) * A, :] = res

    for qb in range(nb + 1):
        finish(qb)


def _attention(proj3, rope_c, rope_s, lam_params, subln_g, seq):
    bsz, n, _ = proj3.shape
    A = ATT_T
    assert seq % A == 0 and n >= N_META + seq
    nb = seq // A
    hb = D_MODEL // 128
    nwin = -(-nb // 2)
    kfeat, qfeat = _mask_features()
    desc = _attn_schedule(nb)
    kern = functools.partial(_attn_kernel, seq=seq, n_entries=desc.shape[0])
    desc = jnp.asarray(desc)
    const = lambda shape: pl.BlockSpec(shape, lambda b, h: (0,) * len(shape))
    return pl.pallas_call(
        kern,
        grid=(bsz, ATTN_HEADS),
        in_specs=[
            pl.BlockSpec(memory_space=pltpu.SMEM),
            pl.BlockSpec((1, n, 128), lambda b, h: (b, 0, COL_Q * hb + h)),
            pl.BlockSpec((1, n, 128), lambda b, h: (b, 0, COL_K * hb + h)),
            pl.BlockSpec((1, n, 128), lambda b, h: (b, 0, COL_V * hb + h)),
            const((n, 128)), const((n, 128)), const((8, 128)), const((1, 128)),
            const((ATT_W, 128)), const((3, 128, A)),
        ],
        out_specs=pl.BlockSpec((1, n, 128), lambda b, h: (b, 0, h)),
        out_shape=jax.ShapeDtypeStruct((bsz, n, D_MODEL), BF16),
        scratch_shapes=[
            pltpu.VMEM((2, nb + 1, 128, A), BF16),
            pltpu.VMEM((nwin * ATT_W, 128), BF16),
            pltpu.VMEM((N_META, 128), BF16),
            pltpu.VMEM((2 * nwin, 128, A), BF16),
            pltpu.VMEM((128, 128), BF16),
            pltpu.VMEM((2, ATT_W, A), F32),
            pltpu.VMEM((2, ATT_W, A), F32),
            pltpu.VMEM((2, ATT_W, A), BF16),
            pltpu.VMEM((2, ATT_W, A), BF16),
            pltpu.VMEM((nb + 1, 2, 1, A), F32),
            pltpu.VMEM((nb + 1, 2, 1, A), F32),
            pltpu.VMEM((nb + 1, 2, 128, A), F32),
        ],
        compiler_params=_cparams(2),
        name="diff_attn",
    )(desc, proj3, proj3, proj3, rope_c, rope_s, lam_params, subln_g, kfeat, qfeat)


def _merge_kernel(h_ref, ys_ref, ya_ref, gs_ref, ga_ref, ws_ref, wa_ref, wo_ref, o_ref):
    a = jnp.dot(ys_ref[...], ws_ref[...], preferred_element_type=F32)
    b = jnp.dot(ya_ref[...], wa_ref[...], preferred_element_type=F32)
    m = _sigmoid(gs_ref[...].astype(F32)) * a + _sigmoid(ga_ref[...].astype(F32)) * b
    o_ref[...] = h_ref[...] + jnp.dot(m.astype(BF16), wo_ref[...], preferred_element_type=F32)


def _merge(h2d, ys2d, ya2d, proj2d, ws, wa, wo):
    m = h2d.shape[0]
    tm = _pick(m, (512, 256, 128))
    row = lambda c: pl.BlockSpec((tm, D_MODEL), lambda i: (i, c))
    wspec = pl.BlockSpec((D_MODEL, D_MODEL), lambda i: (0, 0))
    return pl.pallas_call(
        _merge_kernel,
        grid=(m // tm,),
        in_specs=[row(0), row(0), row(0), row(COL_GS), row(COL_GA), wspec, wspec, wspec],
        out_specs=row(0),
        out_shape=jax.ShapeDtypeStruct((m, D_MODEL), F32),
        compiler_params=_cparams(1),
        name="merge",
    )(h2d, ys2d, ya2d, proj2d, proj2d, ws, wa, wo)


def _mlp_kernel(h_ref, g_ref, wg_ref, wv_ref, cwg_ref, cwv_ref, cbg_ref, cbv_ref, wd_ref, o_ref,
                u_ref, hbuf_a, hbuf_b, carry, act_ref, *, row_chunks):
    i = pl.program_id(1)
    j = pl.program_id(2)
    nj = carry.shape[0]
    tm = u_ref.shape[0]
    rc = tm // row_chunks

    @pl.when(j == 0)
    def _():
        x = h_ref[0]
        ms = jnp.mean(x * x, axis=-1, keepdims=True)
        u_ref[...] = (x * lax.rsqrt(ms + NORM_EPS) * g_ref[...]).astype(BF16)
        o_ref[0] = x

    def body(slot, produce, consume):
        bufs = (hbuf_a, hbuf_b)
        sides = ((cwg_ref, cbg_ref), (cwv_ref, cbv_ref))
        if consume:
            hsrc = bufs[1 - slot if produce else slot]
            jc = j - 1
            for idx in range(2):
                prev = carry[jc, idx]
                hsrc[idx, 0:8, :] = jnp.where(i == 0, jnp.zeros_like(prev), prev)
        for r in range(row_chunks):
            r0 = r * rc
            if produce:
                u = u_ref[r0:r0 + rc, :]
                for idx, w_ref in enumerate((wg_ref, wv_ref)):
                    bufs[slot][idx, 8 + r0:8 + r0 + rc, :] = jnp.dot(u, w_ref[...],
                                                                      preferred_element_type=F32)
            if consume:
                convs = []
                for idx, (cw_ref, cb_ref) in enumerate(sides):
                    conv = cb_ref[...] + cw_ref[2:3, :] * hsrc[idx, 8 + r0:8 + r0 + rc, :]
                    conv = conv + cw_ref[1:2, :] * hsrc[idx, 7 + r0:7 + r0 + rc, :]
                    conv = conv + cw_ref[0:1, :] * hsrc[idx, 6 + r0:6 + r0 + rc, :]
                    convs.append(conv)
                act_ref[r0:r0 + rc, :] = (_silu(convs[0]) * convs[1]).astype(BF16)
                o_ref[0, r0:r0 + rc, :] += jnp.dot(act_ref[r0:r0 + rc, :], wd_ref[...],
                                                   preferred_element_type=F32)
        if consume:
            for idx in range(2):
                carry[jc, idx] = hsrc[idx, tm:tm + 8, :]

    @pl.when(j == 0)
    def _():
        body(0, True, False)

    for slot in range(2):
        @pl.when((j > 0) & (j < nj) & (j % 2 == slot))
        def _():
            body(slot, True, True)

    @pl.when(j == nj)
    def _():
        body((nj - 1) % 2, False, True)


def _mlp(h3, g, w_up, cw, cb, w_down):
    bsz, n, _ = h3.shape
    tm = _pick(n, (1056, 1024, 768, 512, 384, 256, 128))
    nj = D_FF // FF_T
    row_chunks = 3 if tm % 48 == 0 else 1
    up =lambda b, i, j: jnp.minimum(j, nj - 1)
    dn = lambda b, i, j: jnp.maximum(j - 1, 0)
    return pl.pallas_call(
        functools.partial(_mlp_kernel, row_chunks=row_chunks),
        grid=(bsz, n // tm, nj + 1),
        in_specs=[
            pl.BlockSpec((1, tm, D_MODEL), lambda b, i, j: (b, i, 0)),
            pl.BlockSpec((1, D_MODEL), lambda b, i, j: (0, 0)),
            pl.BlockSpec((D_MODEL, FF_T), lambda b, i, j: (0, up(b, i, j))),
            pl.BlockSpec((D_MODEL, FF_T), lambda b, i, j: (0, up(b, i, j) + nj)),
            pl.BlockSpec((MLP_CONV, FF_T), lambda b, i, j: (0, dn(b, i, j))),
            pl.BlockSpec((MLP_CONV, FF_T), lambda b, i, j: (0, dn(b, i, j) + nj)),
            pl.BlockSpec((1, FF_T), lambda b, i, j: (0, dn(b, i, j))),
            pl.BlockSpec((1, FF_T), lambda b, i, j: (0, dn(b, i, j) + nj)),
            pl.BlockSpec((FF_T, D_MODEL), lambda b, i, j: (dn(b, i, j), 0)),
        ],
        out_specs=pl.BlockSpec((1, tm, D_MODEL), lambda b, i, j: (b, i, 0)),
        out_shape=jax.ShapeDtypeStruct((bsz, n, D_MODEL), F32),
        scratch_shapes=[
            pltpu.VMEM((tm, D_MODEL), BF16),
            pltpu.VMEM((2, tm + 8, FF_T), F32),
            pltpu.VMEM((2, tm + 8, FF_T), F32),
            pltpu.VMEM((nj, 2, 8, FF_T), F32),
            pltpu.VMEM((tm, FF_T), BF16),
        ],
        compiler_params=_cparams(3),
        name="convglu_mlp",
    )(h3, g, w_up, w_up, cw, cw, cb, cb, w_down)


def _final_kernel(a_ref, b_ref, g_ref, o_ref):
    tm = o_ref.shape[1]

    def norm(x):
        ms = jnp.mean(x * x, axis=-1, keepdims=True)
        return x * lax.rsqrt(ms + NORM_EPS) * g_ref[...]

    o_ref[0, 0:tm - N_META, :] = norm(a_ref[0, N_META:tm, :])
    o_ref[0, tm - N_META:tm, :] = norm(b_ref[0])


def _final_norm(h3, g, seq):
    bsz = h3.shape[0]
    tm = _pick(seq, (512, 256, 128, 64, 32))
    per = tm // N_META
    return pl.pallas_call(
        _final_kernel,
        grid=(bsz, seq // tm),
        in_specs=[
            pl.BlockSpec((1, tm, D_MODEL), lambda b, i: (b, i, 0)),
            pl.BlockSpec((1, N_META, D_MODEL), lambda b, i: (b, (i + 1) * per, 0)),
            pl.BlockSpec((1, D_MODEL), lambda b, i: (0, 0)),
        ],
        out_specs=pl.BlockSpec((1, tm, D_MODEL), lambda b, i: (b, i, 0)),
        out_shape=jax.ShapeDtypeStruct((bsz, seq, D_MODEL), F32),
        compiler_params=_cparams(2),
        name="final_norm",
    )(h3, h3, g)


def _rope_tables(n_pos):
    half = ROT_DIM // 2
    inv = 1.0 / (ROPE_THETA ** (jnp.arange(half, dtype=F32) * 2.0 / ROT_DIM))
    ang = jnp.arange(n_pos, dtype=F32)[:, None] * inv[None, :]
    cos, sin = jnp.cos(ang), jnp.sin(ang)
    rest = ATTN_HEAD_DIM - ROT_DIM
    c64 = jnp.concatenate([cos, cos, jnp.ones((n_pos, rest), F32)], axis=1)
    s64 = jnp.concatenate([-sin, sin, jnp.zeros((n_pos, rest), F32)], axis=1)
    return jnp.concatenate([c64, c64], axis=1), jnp.concatenate([s64, s64], axis=1)


def _pad_lanes(v, width=128):
    v = v.astype(F32).reshape(1, -1)
    return jnp.pad(v, ((0, 0), (0, width - v.shape[1])))


def kernel(x, meta_tokens, norm1_g, w_in, ssd_conv_w, ssd_conv_b, ssd_dt_bias, ssd_a_log, ssd_d, ssd_norm_g, lambda_q1, lambda_k1, lambda_q2, lambda_k2, attn_subln_g, w_ssd_branch, w_attn_branch, w_out, norm2_g, w_up, mlp_conv_w, mlp_conv_b, w_down, final_norm_g):
    bsz, seq, _ = x.shape
    depth = w_in.shape[0]
    n_tok = N_META + seq
    n = -(-n_tok // Q_BLOCK) * Q_BLOCK
    meta = jnp.broadcast_to(meta_tokens.astype(x.dtype)[None], (bsz, N_META, D_MODEL))
    h = jnp.concatenate([meta, x, jnp.zeros((bsz, n - n_tok, D_MODEL), x.dtype)], axis=1)
    rope_c, rope_s = _rope_tables(n)
    dt_lo = 3 * D_MODEL
    dt_hi = dt_lo + SSD_HEADS

    for l in range(depth):
        lam_init = 0.8 - 0.6 * math.exp(-0.3 * l)
        w_main = jnp.concatenate([w_in[l][:, :dt_lo], w_in[l][:, dt_hi:]], axis=1).astype(BF16)
        w_dt = jnp.pad(w_in[l][:, dt_lo:dt_hi], ((0, 0), (0, 128 - SSD_HEADS))).astype(BF16)
        proj, dt_raw = _inproj(h.reshape(bsz * n, D_MODEL), norm1_g[l].reshape(1, D_MODEL), w_main, w_dt)
        proj3 = proj.reshape(bsz, n, N_PROJ)

        y_ssd = _ssd(proj3, dt_raw.reshape(bsz, n, 128), ssd_conv_w[l], ssd_conv_b[l].reshape(1, -1),
                     _pad_lanes(ssd_dt_bias[l]),
                     jnp.broadcast_to(ssd_a_log[l].astype(F32)[:, None], (SSD_HEADS, SSD_T)),
                     jnp.repeat(ssd_d[l], SSD_HEAD_DIM).reshape(1, D_MODEL),
                     ssd_norm_g[l].reshape(1, D_MODEL))

        lam_params = jnp.concatenate([
            _pad_lanes(lambda_q1[l]), _pad_lanes(lambda_k1[l]), _pad_lanes(lambda_q2[l]),
            _pad_lanes(lambda_k2[l]), jnp.full((1, 128), lam_init, F32), jnp.zeros((3, 128), F32)], axis=0)
        y_attn = _attention(proj3, rope_c, rope_s, lam_params, attn_subln_g[l].reshape(1, 128), seq)

        h2d = _merge(h.reshape(bsz * n, D_MODEL), y_ssd.reshape(bsz * n, D_MODEL),
                     y_attn.reshape(bsz * n, D_MODEL), proj,
                     w_ssd_branch[l].astype(BF16), w_attn_branch[l].astype(BF16), w_out[l].astype(BF16))

        h = _mlp(h2d.reshape(bsz, n, D_MODEL), norm2_g[l].reshape(1, D_MODEL), w_up[l].astype(BF16),
                 mlp_conv_w[l], mlp_conv_b[l].reshape(1, -1), w_down[l].astype(BF16))

    return _final_norm(h, final_norm_g.reshape(1, D_MODEL), seq)
```

```python
import functools
import math

import numpy as np
import jax
import jax.numpy as jnp
from jax import lax
from jax.experimental import pallas as pl
from jax.experimental.pallas import tpu as pltpu

F32 = jnp.float32
BF16 = jnp.bfloat16

D_MODEL = 1024
N_META = 16
CHUNK = 64
Q_BLOCK = 128
NORM_EPS = 1e-6
SUBLN_EPS = 1e-5
SSD_HEADS = 16
SSD_HEAD_DIM = 64
SSD_GROUPS = 4
SSD_STATE = 128
SSD_CONV = 4
ATTN_HEADS = 8
ATTN_HEAD_DIM = 64
ROT_DIM = 16
ROPE_THETA = 500000.0
D_FF = 2816
MLP_CONV = 3

COL_Z, COL_XS, COL_BC, COL_Q, COL_K, COL_V, COL_GS, COL_GA = range(8)
N_PROJ = 8 * D_MODEL

SSD_T = 128
SSD_HALO = 16
FF_T = 256
VMEM_LIMIT = 56 * 1024 * 1024
NEG_BIG = -1e30
LOG2E = 1.4426950408889634

ATT_T = 256
ATT_W = 2 * ATT_T
ATT_RC = 512
ATT_UNROLL = 8
MASK_BIG = 2.0 ** 100
TYPE_PLAIN, TYPE_EVEN, TYPE_ODD = 0, 1, 2


def _cparams(n_axes):
    return pltpu.CompilerParams(dimension_semantics=("arbitrary",) * n_axes,
                                vmem_limit_bytes=VMEM_LIMIT)


def _pick(n, candidates):
    for c in candidates:
        if n % c == 0:
            return c
    raise ValueError(f"no block size for {n}")


def _sigmoid(x):
    return 1.0 / (1.0 + jnp.exp(-x))


def _silu(x):
    return x * _sigmoid(x)


def _inproj_kernel(x_ref, g_ref, w_ref, wdt_ref, o_ref, dt_ref, u_ref):
    @pl.when(pl.program_id(1) == 0)
    def _():
        x = x_ref[...]
        ms = jnp.mean(x * x, axis=-1, keepdims=True)
        u = (x * lax.rsqrt(ms + NORM_EPS) * g_ref[...]).astype(BF16)
        u_ref[...] = u
        dt_ref[...] = jnp.dot(u, wdt_ref[...], preferred_element_type=F32)

    o_ref[...] = jnp.dot(u_ref[...], w_ref[...], preferred_element_type=F32).astype(o_ref.dtype)


def _inproj(h2d, g, w, wdt):
    m = h2d.shape[0]
    tm = _pick(m, (1024, 512, 256, 128))
    tn = 2048
    return pl.pallas_call(
        _inproj_kernel,
        grid=(m // tm, N_PROJ // tn),
        in_specs=[
            pl.BlockSpec((tm, D_MODEL), lambda i, j: (i, 0)),
            pl.BlockSpec((1, D_MODEL), lambda i, j: (0, 0)),
            pl.BlockSpec((D_MODEL, tn), lambda i, j: (0, j)),
            pl.BlockSpec((D_MODEL, 128), lambda i, j: (0, 0)),
        ],
        out_specs=[
            pl.BlockSpec((tm, tn), lambda i, j: (i, j)),
            pl.BlockSpec((tm, 128), lambda i, j: (i, 0)),
        ],
        out_shape=[
            jax.ShapeDtypeStruct((m, N_PROJ), BF16),
            jax.ShapeDtypeStruct((m, 128), F32),
        ],
        scratch_shapes=[pltpu.VMEM((tm, D_MODEL), BF16)],
        compiler_params=_cparams(2),
        name="inproj",
    )(h2d, g, w, wdt)


def _conv_shift_matrices():
    t = np.arange(SSD_T)
    sh = np.zeros((SSD_CONV - 1, SSD_T, SSD_HALO + SSD_T), np.float32)
    for s in range(1, SSD_CONV):
        sh[s - 1, t, SSD_HALO + t - s] = 1.0
    return jnp.asarray(sh, BF16)


def _ssd_kernel(xs_ref, bc_ref, z_ref, dt_ref, cw_ref, cb_ref, dtb_ref, alog_ref, dsk_ref, ng_ref, sh_ref,
                y_ref, halo, xact, ybuf, state):
    T = SSD_T
    W = 2 * D_MODEL

    @pl.when(pl.program_id(1) == 0)
    def _():
        halo[...] = jnp.zeros(halo.shape, BF16)
        state[...] = jnp.zeros(state.shape, F32)

    CW = 512
    for c in range(W // CW):
        cs = slice(c * CW, (c + 1) * CW)
        src, off = (xs_ref, c * CW) if c * CW < D_MODEL else (bc_ref, c * CW - D_MODEL)
        cur = src[0, :, off:off + CW]
        ext = jnp.concatenate([halo[:, cs], cur], axis=0)
        acc = cb_ref[:, cs] + cw_ref[SSD_CONV - 1:SSD_CONV, cs] * cur.astype(F32)
        for s in range(1, SSD_CONV):
            k = SSD_CONV - 1 - s
            acc = acc + cw_ref[k:k + 1, cs] * jnp.dot(sh_ref[s - 1], ext, preferred_element_type=F32)
        xact[:, cs] = _silu(acc)
        halo[:, cs] = cur[T - SSD_HALO:T, :]

    dtv = (dt_ref[0] + dtb_ref[...]).T[0:SSD_HEADS, :]
    dt_t = jnp.maximum(dtv, 0.0) + jnp.log1p(jnp.exp(-jnp.abs(dtv)))
    adt_t = dt_t * (-jnp.exp(alog_ref[...]))
    row = lax.broadcasted_iota(jnp.int32, (T, T), 0)
    col = lax.broadcasted_iota(jnp.int32, (T, T), 1)
    causal = row >= col
    a_cs_t = jnp.dot(adt_t, (row <= col).astype(F32), preferred_element_type=F32,
                     precision=lax.Precision.HIGHEST)
    a_cs = jnp.concatenate([a_cs_t, jnp.zeros((T - SSD_HEADS, T), F32)], axis=0).T
    lo_lane = col < SSD_HEAD_DIM
    lo_row = row < SSD_HEAD_DIM

    nt_dims = (((1,), (1,)), ((), ()))
    for g in range(SSD_GROUPS):
        b_off = D_MODEL + SSD_STATE * g
        c_off = D_MODEL + SSD_GROUPS * SSD_STATE + SSD_STATE * g
        bg = xact[:, b_off:b_off + SSD_STATE].astype(BF16)
        cg = xact[:, c_off:c_off + SSD_STATE].astype(BF16)
        cb = lax.dot_general(cg, bg, nt_dims, preferred_element_type=F32)
        for pr in range(2):
            h0 = 4 * g + 2 * pr
            ps = slice(128 * (2 * g + pr), 128 * (2 * g + pr + 1))
            xs_pair = xact[:, ps]
            mats = []
            cols = []
            for hh in (h0, h0 + 1):
                colb = jnp.broadcast_to(a_cs[:, hh:hh + 1], (T, T))
                rowb = a_cs_t[hh:hh + 1, :]
                decay = jnp.exp(jnp.where(causal, colb - rowb, -jnp.inf))
                mats.append((cb * decay * dt_t[hh:hh + 1, :]).astype(BF16))
                cols.append(colb)
            x_lo = jnp.where(lo_lane, xs_pair, 0.0).astype(BF16)
            x_hi = jnp.where(lo_lane, 0.0, xs_pair).astype(BF16)
            y_diag = (jnp.dot(mats[0], x_lo, preferred_element_type=F32)
                      + jnp.dot(mats[1], x_hi, preferred_element_type=F32))
            st = state[ps, :]
            y_off = lax.dot_general(cg, st.astype(BF16), nt_dims, preferred_element_type=F32)
            y_off = y_off * jnp.exp(jnp.where(lo_lane, cols[0], cols[1]))
            ybuf[:, ps] = y_diag + y_off + xs_pair * dsk_ref[:, ps]

            last0 = a_cs_t[h0:h0 + 1, T - 1:T]
            last1 = a_cs_t[h0 + 1:h0 + 2, T - 1:T]
            w0 = jnp.exp(last0 - a_cs_t[h0:h0 + 1, :]) * dt_t[h0:h0 + 1, :]
            w1 = jnp.exp(last1 - a_cs_t[h0 + 1:h0 + 2, :]) * dt_t[h0 + 1:h0 + 2, :]
            wmat = jnp.where(lo_row, w0, w1)
            upd = jnp.dot((xs_pair.T * wmat).astype(BF16), bg, preferred_element_type=F32)
            cdec = jnp.where(lo_row, jnp.exp(last0), jnp.exp(last1))
            state[ps, :] = st * cdec + upd

    zz = z_ref[0].astype(F32)
    gw = D_MODEL // SSD_GROUPS
    for g in range(SSD_GROUPS):
        gs = slice(g * gw, (g + 1) * gw)
        y = ybuf[:, gs] * _silu(zz[:, gs])
        ms = jnp.mean(y * y, axis=-1, keepdims=True)
        y_ref[0, :, gs] = (y * lax.rsqrt(ms + NORM_EPS) * ng_ref[:, gs]).astype(y_ref.dtype)


def _ssd(proj3, dt3, cw, cb, dtb, alog, dsk, ng):
    bsz, n, _ = proj3.shape
    T = SSD_T
    vec = lambda w: pl.BlockSpec((1, w), lambda b, t: (0, 0))
    shifts = _conv_shift_matrices()
    return pl.pallas_call(
        _ssd_kernel,
        grid=(bsz, n // T),
        in_specs=[
            pl.BlockSpec((1, T, D_MODEL), lambda b, t: (b, t, COL_XS)),
            pl.BlockSpec((1, T, D_MODEL), lambda b, t: (b, t, COL_BC)),
            pl.BlockSpec((1, T, D_MODEL), lambda b, t: (b, t, COL_Z)),
            pl.BlockSpec((1, T, 128), lambda b, t: (b, t, 0)),
            pl.BlockSpec((SSD_CONV, 2 * D_MODEL), lambda b, t: (0, 0)),
            vec(2 * D_MODEL), vec(128),
            pl.BlockSpec((SSD_HEADS, T), lambda b, t: (0, 0)),
            vec(D_MODEL), vec(D_MODEL),
            pl.BlockSpec(shifts.shape, lambda b, t: (0, 0, 0)),
        ],
        out_specs=pl.BlockSpec((1, T, D_MODEL), lambda b, t: (b, t, 0)),
        out_shape=jax.ShapeDtypeStruct((bsz, n, D_MODEL), BF16),
        scratch_shapes=[
            pltpu.VMEM((SSD_HALO, 2 * D_MODEL), BF16),
            pltpu.VMEM((T, 2 * D_MODEL), F32),
            pltpu.VMEM((T, D_MODEL), F32),
            pltpu.VMEM((SSD_HEADS * SSD_HEAD_DIM, SSD_STATE), F32),
        ],
        compiler_params=_cparams(2),
        name="ssd",
    )(proj3, proj3, proj3, dt3, cw, cb, dtb, alog, dsk, ng, shifts)


def _rope(t, c, s):
    lane = lax.broadcasted_iota(jnp.int32, t.shape, 1)
    first_half = (lane % ATTN_HEAD_DIM) < (ROT_DIM // 2)
    partner = jnp.where(first_half, pltpu.roll(t, 128 - ROT_DIM // 2, axis=1),
                        pltpu.roll(t, ROT_DIM // 2, axis=1))
    return t * c + partner * s


def _mask_features():
    r = np.arange(ATT_W)
    kf = np.zeros((ATT_W, 128), np.float32)
    kf[r, r // CHUNK] = 1.0
    qc = np.arange(ATT_T) // CHUNK
    qf = np.zeros((3, 128, ATT_T), np.float32)
    for c in range(ATT_W // CHUNK):
        qf[TYPE_EVEN, c, :] = np.where(c > qc, -MASK_BIG, 0.0)
        qf[TYPE_ODD, c, :] = np.where(c > qc + ATT_T // CHUNK, -MASK_BIG, 0.0)
    return jnp.asarray(kf, BF16), jnp.asarray(qf, BF16)


def _attn_schedule(nb):
    rows = []
    for b in range(nb):
        for w in range(b // 2 + 1):
            rows.append((b, w, TYPE_PLAIN if w < b // 2 else TYPE_EVEN + b % 2, 0))
    return np.asarray(rows, np.int32)


def _attn_kernel(desc_ref, q_ref, k_ref, v_ref, c_ref, s_ref, lp_ref, g_ref, kf_ref, qf_ref, o_ref,
                 qz, ks, km, vt, vtm, sa, sb, pa, pb, m_all, l_all, acc_all, *, seq, n_entries):
    A = ATT_T
    W = ATT_W
    RC = ATT_RC
    n = o_ref.shape[1]
    nb = seq // A
    lo_row = lax.broadcasted_iota(jnp.int32, (128, A), 0) < ATTN_HEAD_DIM
    qscale = ATTN_HEAD_DIM ** -0.5 * LOG2E

    def load(ref, rows):
        return ref[0, rows, :].astype(F32)

    def store_q(j, qf):
        qt = qf.T
        qz[0, j] = jnp.where(lo_row, qt, 0.0).astype(BF16)
        qz[1, j] = jnp.where(lo_row, 0.0, qt).astype(BF16)

    mrows = slice(0, N_META)
    cm, sm = c_ref[mrows, :], s_ref[mrows, :]
    km[...] = _rope(load(k_ref, mrows), cm, sm).astype(BF16)
    q_meta = _rope(load(q_ref, mrows), cm, sm) * qscale
    store_q(nb, jnp.concatenate([q_meta, jnp.zeros((A - N_META, 128), F32)], axis=0))
    v_meta = jnp.concatenate([load(v_ref, mrows), jnp.zeros((128 - N_META, 128), F32)], axis=0)
    vtm[...] = v_meta.T.astype(BF16)
    o_ref[0, N_META + seq:n, :] = jnp.zeros((n - N_META - seq, 128), o_ref.dtype)
    m_all[...] = jnp.full(m_all.shape, NEG_BIG, F32)
    l_all[...] = jnp.zeros(l_all.shape, F32)
    acc_all[...] = jnp.zeros(acc_all.shape, F32)
    if nb % 2:
        ks[nb * A:(nb + 1) * A, :] = jnp.zeros((A, 128), BF16)
        vt[nb] = jnp.zeros((128, A), BF16)

    def prep_block(j, carry):
        rows = pl.ds(pl.multiple_of(N_META + j * A, 16), A)
        c, s = c_ref[rows, :], s_ref[rows, :]
        store_q(j, _rope(load(q_ref, rows), c, s) * qscale)
        ks[pl.ds(pl.multiple_of(j * A, A), A), :] = _rope(load(k_ref, rows), c, s).astype(BF16)
        vt[j] = load(v_ref, rows).T.astype(BF16)
        return carry

    lax.fori_loop(0, nb, prep_block, 0)

    lp = lp_ref[...]
    lam_init = lp[4:5, 0:1]
    lam = (jnp.exp(jnp.sum(lp[0:1] * lp[1:2], axis=1, keepdims=True))
           - jnp.exp(jnp.sum(lp[2:3] * lp[3:4], axis=1, keepdims=True)) + lam_init)
    out_gain = g_ref[...] * (1.0 - lam_init)

    def step(t, par, mx, al):
        has_y = isinstance(t, jax.Array) or t < n_entries
        has_x = isinstance(t, jax.Array) or 1 <= t <= n_entries
        has_z = isinstance(t, jax.Array) or 2 <= t <= n_entries + 1
        s_y, s_x = (sa, sb) if par == 0 else (sb, sa)
        p_z, p_x = (pa, pb) if par == 0 else (pb, pa)
        new_mx, new_al = mx, al
        if has_z:
            qbz, wz = desc_ref[t - 2, 0], desc_ref[t - 2, 1]
            for m in range(2):
                pv = (jnp.dot(vt[2 * wz], p_z[m, 0:A, :], preferred_element_type=F32)
                      + jnp.dot(vt[2 * wz + 1], p_z[m, A:W, :], preferred_element_type=F32))
                acc_all[qbz, m] = al[m] * acc_all[qbz, m] + pv
        if has_x:
            qbx = desc_ref[t - 1, 0]
            m_old = [m_all[qbx, m] for m in range(2)]
            m_new = [jnp.maximum(m_old[m], mx[m]) for m in range(2)]
            new_al = tuple(jnp.exp2(m_old[m] - m_new[m]) for m in range(2))
            psum = [[], []]
        if has_y:
            qby, wy = desc_ref[t, 0], desc_ref[t, 1]
            feat = qf_ref[desc_ref[t, 2]]
            rhs = [jnp.concatenate([qz[m, qby], feat], axis=0) for m in range(2)]
            ymax = [[], []]
        for c in range(W // RC):
            rows = slice(c * RC, (c + 1) * RC)
            if has_y:
                krows = pl.ds(pl.multiple_of(wy * W + c * RC, 16), RC)
                kc = jnp.concatenate([ks[krows, :], kf_ref[rows, :]], axis=1)
            for m in range(2):
                if has_x:
                    p = jnp.exp2(s_x[m, rows, :] - m_new[m])
                    psum[m].append(jnp.sum(p, axis=0, keepdims=True))
                    p_x[m, rows, :] = p.astype(BF16)
                if has_y:
                    s = jnp.dot(kc, rhs[m], preferred_element_type=F32)
                    s_y[m, rows, :] = s
                    ymax[m].append(jnp.max(s, axis=0, keepdims=True))
        if has_x:
            for m in range(2):
                l_all[qbx, m] = new_al[m] * l_all[qbx, m] + functools.reduce(jnp.add, psum[m])
                m_all[qbx, m] = m_new[m]
        if has_y:
            new_mx = tuple(functools.reduce(jnp.maximum, ymax[m]) for m in range(2))
        return new_mx, new_al

    row0 = jnp.zeros((1, A), F32)
    mx, al = (row0, row0), (row0, row0)
    for t in (0, 1):
        mx, al = step(t, t % 2, mx, al)

    def group(u, carry):
        mx, al = carry[0:2], carry[2:4]
        for k in range(ATT_UNROLL):
            mx, al = step(2 + ATT_UNROLL * u + k, k % 2, mx, al)
        return mx + al

    n_mid = max(n_entries - 2, 0)
    carry = lax.fori_loop(0, n_mid // ATT_UNROLL, group, mx + al)
    mx, al = carry[0:2], carry[2:4]
    for t in range(2 + ATT_UNROLL * (n_mid // ATT_UNROLL), n_entries + 2):
        mx, al = step(t, t % 2, mx, al)

    kme = jnp.concatenate([km[...], jnp.zeros((N_META, 128), BF16)], axis=1)

    def finish(qb):
        outs = []
        for m in range(2):
            rhs = jnp.concatenate([qz[m, qb], qf_ref[TYPE_PLAIN]], axis=0)
            s = jnp.dot(kme, rhs, preferred_element_type=F32)
            m_old = m_all[qb, m]
            m_new = jnp.maximum(m_old, jnp.max(s, axis=0, keepdims=True))
            alpha = jnp.exp2(m_old - m_new)
            p = jnp.exp2(s - m_new)
            l = alpha * l_all[qb, m] + jnp.sum(p, axis=0, keepdims=True)
            pp = jnp.concatenate([p.astype(BF16), jnp.zeros((128 - N_META, A), BF16)], axis=0)
            acc = alpha * acc_all[qb, m] + jnp.dot(vtm[...], pp, preferred_element_type=F32)
            outs.append(acc * (1.0 / l))
        ot = (outs[0] - lam * outs[1]).T
        ms = jnp.mean(ot * ot, axis=-1, keepdims=True)
        res = (ot * lax.rsqrt(ms + SUBLN_EPS) * out_gain).astype(o_ref.dtype)

        if qb == nb:
            o_ref[0, 0:N_META, :] = res[0:N_META]
        else:
            o_ref[0, N_META + qb * A:N_META + (qb + 1) * A, :] = res

    for qb in range(nb + 1):
        finish(qb)


def _attention(proj3, rope_c, rope_s, lam_params, subln_g, seq):
    bsz, n, _ = proj3.shape
    A = ATT_T
    assert seq % A == 0 and n >= N_META + seq
    nb = seq // A
    hb = D_MODEL // 128
    nwin = -(-nb // 2)
    kfeat, qfeat = _mask_features()
    desc = _attn_schedule(nb)
    kern = functools.partial(_attn_kernel, seq=seq, n_entries=desc.shape[0])
    desc = jnp.asarray(desc)
    const = lambda shape: pl.BlockSpec(shape, lambda b, h: (0,) * len(shape))
    return pl.pallas_call(
        kern,
        grid=(bsz, ATTN_HEADS),
        in_specs=[
            pl.BlockSpec(memory_space=pltpu.SMEM),
            pl.BlockSpec((1, n, 128), lambda b, h: (b, 0, COL_Q * hb + h)),
            pl.BlockSpec((1, n, 128), lambda b, h: (b, 0, COL_K * hb + h)),
            pl.BlockSpec((1, n, 128), lambda b, h: (b, 0, COL_V * hb + h)),
            const((n, 128)), const((n, 128)), const((8, 128)), const((1, 128)),
            const((ATT_W, 128)), const((3, 128, A)),
        ],
        out_specs=pl.BlockSpec((1, n, 128), lambda b, h: (b, 0, h)),
        out_shape=jax.ShapeDtypeStruct((bsz, n, D_MODEL), BF16),
        scratch_shapes=[
            pltpu.VMEM((2, nb + 1, 128, A), BF16),
            pltpu.VMEM((nwin * ATT_W, 128), BF16),
            pltpu.VMEM((N_META, 128), BF16),
            pltpu.VMEM((2 * nwin, 128, A), BF16),
            pltpu.VMEM((128, 128), BF16),
            pltpu.VMEM((2, ATT_W, A), F32),
            pltpu.VMEM((2, ATT_W, A), F32),
            pltpu.VMEM((2, ATT_W, A), BF16),
            pltpu.VMEM((2, ATT_W, A), BF16),
            pltpu.VMEM((nb + 1, 2, 1, A), F32),
            pltpu.VMEM((nb + 1, 2, 1, A), F32),
            pltpu.VMEM((nb + 1, 2, 128, A), F32),
        ],
        compiler_params=_cparams(2),
        name="diff_attn",
    )(desc, proj3, proj3, proj3, rope_c, rope_s, lam_params, subln_g, kfeat, qfeat)


def _merge_kernel(h_ref, ys_ref, ya_ref, gs_ref, ga_ref, ws_ref, wa_ref, wo_ref, o_ref):
    a = jnp.dot(ys_ref[...], ws_ref[...], preferred_element_type=F32)
    b = jnp.dot(ya_ref[...], wa_ref[...], preferred_element_type=F32)
    m = _sigmoid(gs_ref[...].astype(F32)) * a + _sigmoid(ga_ref[...].astype(F32)) * b
    o_ref[...] = h_ref[...] + jnp.dot(m.astype(BF16), wo_ref[...], preferred_element_type=F32)


def _merge(h2d, ys2d, ya2d, proj2d, ws, wa, wo):
    m = h2d.shape[0]
    tm = _pick(m, (512, 256, 128))
    row = lambda c: pl.BlockSpec((tm, D_MODEL), lambda i: (i, c))
    wspec = pl.BlockSpec((D_MODEL, D_MODEL), lambda i: (0, 0))
    return pl.pallas_call(
        _merge_kernel,
        grid=(m // tm,),
        in_specs=[row(0), row(0), row(0), row(COL_GS), row(COL_GA), wspec, wspec, wspec],
        out_specs=row(0),
        out_shape=jax.ShapeDtypeStruct((m, D_MODEL), F32),
        compiler_params=_cparams(1),
        name="merge",
    )(h2d, ys2d, ya2d, proj2d, proj2d, ws, wa, wo)


def _mlp_kernel(h_ref, g_ref, wg_ref, wv_ref, cwg_ref, cwv_ref, cbg_ref, cbv_ref, wd_ref, o_ref,
                u_ref, hbuf, carry):
    i = pl.program_id(1)
    j = pl.program_id(2)
    tm = u_ref.shape[0]

    @pl.when(j == 0)
    def _():
        x = h_ref[0]
        ms = jnp.mean(x * x, axis=-1, keepdims=True)
        u_ref[...] = (x * lax.rsqrt(ms + NORM_EPS) * g_ref[...]).astype(BF16)
        o_ref[0] = x

    u = u_ref[...]
    convs = []
    for idx, (w_ref, cw_ref, cb_ref) in enumerate(((wg_ref, cwg_ref, cbg_ref), (wv_ref, cwv_ref, cbv_ref))):
        hid = jnp.dot(u, w_ref[...], preferred_element_type=F32)
        prev = carry[j, idx]
        hbuf[idx, 0:8, :] = jnp.where(i == 0, jnp.zeros_like(prev), prev)
        hbuf[idx, 8:8 + tm, :] = hid
        conv = cb_ref[...] + cw_ref[2:3, :] * hbuf[idx, 8:8 + tm, :]
        conv = conv + cw_ref[1:2, :] * hbuf[idx, 7:7 + tm, :]
        conv = conv + cw_ref[0:1, :] * hbuf[idx, 6:6 + tm, :]
        carry[j, idx] = hbuf[idx, tm:tm + 8, :]
        convs.append(conv)
    act = (_silu(convs[0]) * convs[1]).astype(BF16)
    o_ref[0] += jnp.dot(act, wd_ref[...], preferred_element_type=F32)


def _mlp(h3, g, w_up, cw, cb, w_down):
    bsz, n, _ = h3.shape
    tm = _pick(n, (2112, 1056, 1024, 768, 512, 384, 256, 128))
    nj = D_FF // FF_T
    return pl.pallas_call(
        _mlp_kernel,
        grid=(bsz, n // tm, nj),
        in_specs=[
            pl.BlockSpec((1, tm, D_MODEL), lambda b, i, j: (b, i, 0)),
            pl.BlockSpec((1, D_MODEL), lambda b, i, j: (0, 0)),
            pl.BlockSpec((D_MODEL, FF_T), lambda b, i, j: (0, j)),
            pl.BlockSpec((D_MODEL, FF_T), lambda b, i, j: (0, j + nj)),
            pl.BlockSpec((MLP_CONV, FF_T), lambda b, i, j: (0, j)),
            pl.BlockSpec((MLP_CONV, FF_T), lambda b, i, j: (0, j + nj)),
            pl.BlockSpec((1, FF_T), lambda b, i, j: (0, j)),
            pl.BlockSpec((1, FF_T), lambda b, i, j: (0, j + nj)),
            pl.BlockSpec((FF_T, D_MODEL), lambda b, i, j: (j, 0)),
        ],
        out_specs=pl.BlockSpec((1, tm, D_MODEL), lambda b, i, j: (b, i, 0)),
        out_shape=jax.ShapeDtypeStruct((bsz, n, D_MODEL), F32),
        scratch_shapes=[
            pltpu.VMEM((tm, D_MODEL), BF16),
            pltpu.VMEM((2, tm + 8, FF_T), F32),
            pltpu.VMEM((nj, 2, 8, FF_T), F32),
        ],
        compiler_params=_cparams(3),
        name="convglu_mlp",
    )(h3, g, w_up, w_up, cw, cw, cb, cb, w_down)


def _final_kernel(a_ref, b_ref, g_ref, o_ref):
    tm = o_ref.shape[1]

    def norm(x):
        ms = jnp.mean(x * x, axis=-1, keepdims=True)
        return x * lax.rsqrt(ms + NORM_EPS) * g_ref[...]

    o_ref[0, 0:tm - N_META, :] = norm(a_ref[0, N_META:tm, :])
    o_ref[0, tm - N_META:tm, :] = norm(b_ref[0])


def _final_norm(h3, g, seq):
    bsz = h3.shape[0]
    tm = _pick(seq, (512, 256, 128, 64, 32))
    per = tm // N_META
    return pl.pallas_call(
        _final_kernel,
        grid=(bsz, seq // tm),
        in_specs=[
            pl.BlockSpec((1, tm, D_MODEL), lambda b, i: (b, i, 0)),
            pl.BlockSpec((1, N_META, D_MODEL), lambda b, i: (b, (i + 1) * per, 0)),
            pl.BlockSpec((1, D_MODEL), lambda b, i: (0, 0)),
        ],
        out_specs=pl.BlockSpec((1, tm, D_MODEL), lambda b, i: (b, i, 0)),
        out_shape=jax.ShapeDtypeStruct((bsz, seq, D_MODEL), F32),
        compiler_params=_cparams(2),
        name="final_norm",
    )(h3, h3, g)


def _rope_tables(n_pos):
    half = ROT_DIM // 2
    inv = 1.0 / (ROPE_THETA ** (jnp.arange(half, dtype=F32) * 2.0 / ROT_DIM))
    ang = jnp.arange(n_pos, dtype=F32)[:, None] * inv[None, :]
    cos, sin = jnp.cos(ang), jnp.sin(ang)
    rest = ATTN_HEAD_DIM - ROT_DIM
    c64 = jnp.concatenate([cos, cos, jnp.ones((n_pos, rest), F32)], axis=1)
    s64 = jnp.concatenate([-sin, sin, jnp.zeros((n_pos, rest), F32)], axis=1)
    return jnp.concatenate([c64, c64], axis=1), jnp.concatenate([s64, s64], axis=1)


def _pad_lanes(v, width=128):
    v = v.astype(F32).reshape(1, -1)
    return jnp.pad(v, ((0, 0), (0, width - v.shape[1])))


def kernel(x, meta_tokens, norm1_g, w_in, ssd_conv_w, ssd_conv_b, ssd_dt_bias, ssd_a_log, ssd_d, ssd_norm_g, lambda_q1, lambda_k1, lambda_q2, lambda_k2, attn_subln_g, w_ssd_branch, w_attn_branch, w_out, norm2_g, w_up, mlp_conv_w, mlp_conv_b, w_down, final_norm_g):
    bsz, seq, _ = x.shape
    depth = w_in.shape[0]
    n_tok = N_META + seq
    n = -(-n_tok // Q_BLOCK) * Q_BLOCK
    meta = jnp.broadcast_to(meta_tokens.astype(x.dtype)[None], (bsz, N_META, D_MODEL))
    h = jnp.concatenate([meta, x, jnp.zeros((bsz, n - n_tok, D_MODEL), x.dtype)], axis=1)
    rope_c, rope_s = _rope_tables(n)
    dt_lo = 3 * D_MODEL
    dt_hi = dt_lo + SSD_HEADS

    for l in range(depth):
        lam_init = 0.8 - 0.6 * math.exp(-0.3 * l)
        w_main = jnp.concatenate([w_in[l][:, :dt_lo], w_in[l][:, dt_hi:]], axis=1).astype(BF16)
        w_dt = jnp.pad(w_in[l][:, dt_lo:dt_hi], ((0, 0), (0, 128 - SSD_HEADS))).astype(BF16)
        proj, dt_raw = _inproj(h.reshape(bsz * n, D_MODEL), norm1_g[l].reshape(1, D_MODEL), w_main, w_dt)
        proj3 = proj.reshape(bsz, n, N_PROJ)

        y_ssd = _ssd(proj3, dt_raw.reshape(bsz, n, 128), ssd_conv_w[l], ssd_conv_b[l].reshape(1, -1),
                     _pad_lanes(ssd_dt_bias[l]),
                     jnp.broadcast_to(ssd_a_log[l].astype(F32)[:, None], (SSD_HEADS, SSD_T)),
                     jnp.repeat(ssd_d[l], SSD_HEAD_DIM).reshape(1, D_MODEL),
                     ssd_norm_g[l].reshape(1, D_MODEL))

        lam_params = jnp.concatenate([
            _pad_lanes(lambda_q1[l]), _pad_lanes(lambda_k1[l]), _pad_lanes(lambda_q2[l]),
            _pad_lanes(lambda_k2[l]), jnp.full((1, 128), lam_init, F32), jnp.zeros((3, 128), F32)], axis=0)
        y_attn = _attention(proj3, rope_c, rope_s, lam_params, attn_subln_g[l].reshape(1, 128), seq)

        h2d = _merge(h.reshape(bsz * n, D_MODEL), y_ssd.reshape(bsz * n, D_MODEL),
                     y_attn.reshape(bsz * n, D_MODEL), proj,
                     w_ssd_branch[l].astype(BF16), w_attn_branch[l].astype(BF16), w_out[l].astype(BF16))

        h = _mlp(h2d.reshape(bsz, n, D_MODEL), norm2_g[l].reshape(1, D_MODEL), w_up[l].astype(BF16),
                 mlp_conv_w[l], mlp_conv_b[l].reshape(1, -1), w_down[l].astype(BF16))

    return _final_norm(h, final_norm_g.reshape(1, D_MODEL), seq)
```

```python
import functools
import math

import numpy as np
import jax
import jax.numpy as jnp
from jax import lax
from jax.experimental import pallas as pl
from jax.experimental.pallas import tpu as pltpu

F32 = jnp.float32
BF16 = jnp.bfloat16

D_MODEL = 1024
N_META = 16
CHUNK = 64
Q_BLOCK = 128
NORM_EPS = 1e-6
SUBLN_EPS = 1e-5
SSD_HEADS = 16
SSD_HEAD_DIM = 64
SSD_GROUPS = 4
SSD_STATE = 128
SSD_CONV = 4
ATTN_HEADS = 8
ATTN_HEAD_DIM = 64
ROT_DIM = 16
ROPE_THETA = 500000.0
D_FF = 2816
MLP_CONV = 3

COL_Z, COL_XS, COL_BC, COL_Q, COL_K, COL_V, COL_GS, COL_GA = range(8)
N_PROJ = 8 * D_MODEL

SSD_T = 128
SSD_HALO = 16
FF_T = 256
VMEM_LIMIT = 56 * 1024 * 1024
NEG_BIG = -1e30
LOG2E = 1.4426950408889634

ATT_T = 256
ATT_W = 2 * ATT_T
ATT_RC = 512
ATT_UNROLL = 8
MASK_BIG = 2.0 ** 100
TYPE_PLAIN, TYPE_EVEN, TYPE_ODD = 0, 1, 2


def _cparams(n_axes):
    return pltpu.CompilerParams(dimension_semantics=("arbitrary",) * n_axes,
                                vmem_limit_bytes=VMEM_LIMIT)


def _pick(n, candidates):
    for c in candidates:
        if n % c == 0:
            return c
    raise ValueError(f"no block size for {n}")


def _sigmoid(x):
    return 1.0 / (1.0 + jnp.exp(-x))


def _silu(x):
    return x * _sigmoid(x)


def _inproj_kernel(x_ref, g_ref, w_ref, wdt_ref, o_ref, dt_ref, u_ref):
    @pl.when(pl.program_id(1) == 0)
    def _():
        x = x_ref[...]
        ms = jnp.mean(x * x, axis=-1, keepdims=True)
        u = (x * lax.rsqrt(ms + NORM_EPS) * g_ref[...]).astype(BF16)
        u_ref[...] = u
        dt_ref[...] = jnp.dot(u, wdt_ref[...], preferred_element_type=F32)

    o_ref[...] = jnp.dot(u_ref[...], w_ref[...], preferred_element_type=F32).astype(o_ref.dtype)


def _inproj(h2d, g, w, wdt):
    m = h2d.shape[0]
    tm = _pick(m, (1024, 512, 256, 128))
    tn = 4096
    return pl.pallas_call(
        _inproj_kernel,
        grid=(m // tm, N_PROJ // tn),
        in_specs=[
            pl.BlockSpec((tm, D_MODEL), lambda i, j: (i, 0)),
            pl.BlockSpec((1, D_MODEL), lambda i, j: (0, 0)),
            pl.BlockSpec((D_MODEL, tn), lambda i, j: (0, j)),
            pl.BlockSpec((D_MODEL, 128), lambda i, j: (0, 0)),
        ],
        out_specs=[
            pl.BlockSpec((tm, tn), lambda i, j: (i, j)),
            pl.BlockSpec((tm, 128), lambda i, j: (i, 0)),
        ],
        out_shape=[
            jax.ShapeDtypeStruct((m, N_PROJ), BF16),
            jax.ShapeDtypeStruct((m, 128), F32),
        ],
        scratch_shapes=[pltpu.VMEM((tm, D_MODEL), BF16)],
        compiler_params=_cparams(2),
        name="inproj",
    )(h2d, g, w, wdt)


def _conv_shift_matrices():
    t = np.arange(SSD_T)
    sh = np.zeros((SSD_CONV - 1, SSD_T, SSD_HALO + SSD_T), np.float32)
    for s in range(1, SSD_CONV):
        sh[s - 1, t, SSD_HALO + t - s] = 1.0
    return jnp.asarray(sh, BF16)


def _ssd_kernel(xs_ref, bc_ref, z_ref, dt_ref, cw_ref, cb_ref, dtb_ref, alog_ref, dsk_ref, ng_ref, sh_ref,
                y_ref, halo, xact, ybuf, state):
    @pl.when(pl.program_id(1) == 0)
    def _():
        halo[...] = jnp.zeros(halo.shape, BF16)
        state[...] = jnp.zeros(state.shape, F32)

    for sub in range(y_ref.shape[1] // SSD_T):
        _ssd_chunk(sub * SSD_T, xs_ref, bc_ref, z_ref, dt_ref, cw_ref, cb_ref, dtb_ref, alog_ref, dsk_ref,
                   ng_ref, sh_ref, y_ref, halo, xact, ybuf, state)


def _ssd_chunk(r0, xs_ref, bc_ref, z_ref, dt_ref, cw_ref, cb_ref, dtb_ref, alog_ref, dsk_ref, ng_ref, sh_ref,
               y_ref, halo, xact, ybuf, state):
    T = SSD_T
    W = 2 * D_MODEL

    CW = 512
    for c in range(W // CW):
        cs = slice(c * CW, (c + 1) * CW)
        src, off = (xs_ref, c * CW) if c * CW < D_MODEL else (bc_ref, c * CW - D_MODEL)
        cur = src[0, r0:r0 + T, off:off + CW]
        ext = jnp.concatenate([halo[:, cs], cur], axis=0)
        acc = cb_ref[:, cs] + cw_ref[SSD_CONV - 1:SSD_CONV, cs] * cur.astype(F32)
        for s in range(1, SSD_CONV):
            k = SSD_CONV - 1 - s
            acc = acc + cw_ref[k:k + 1, cs] * jnp.dot(sh_ref[s - 1], ext, preferred_element_type=F32)
        xact[:, cs] = _silu(acc)
        halo[:, cs] = cur[T - SSD_HALO:T, :]

    dtv = (dt_ref[0, r0:r0 + T, :] + dtb_ref[...]).T[0:SSD_HEADS, :]
    dt_t = jnp.maximum(dtv, 0.0) + jnp.log1p(jnp.exp(-jnp.abs(dtv)))
    adt_t = dt_t * (-jnp.exp(alog_ref[...]))
    row = lax.broadcasted_iota(jnp.int32, (T, T), 0)
    col = lax.broadcasted_iota(jnp.int32, (T, T), 1)
    causal = row >= col
    a_cs_t = jnp.dot(adt_t, (row <= col).astype(F32), preferred_element_type=F32,
                     precision=lax.Precision.HIGHEST)
    a_cs = jnp.concatenate([a_cs_t, jnp.zeros((T - SSD_HEADS, T), F32)], axis=0).T
    lo_lane = col < SSD_HEAD_DIM
    lo_row = row < SSD_HEAD_DIM

    nt_dims = (((1,), (1,)), ((), ()))
    for g in range(SSD_GROUPS):
        b_off = D_MODEL + SSD_STATE * g
        c_off = D_MODEL + SSD_GROUPS * SSD_STATE + SSD_STATE * g
        bg = xact[:, b_off:b_off + SSD_STATE].astype(BF16)
        cg = xact[:, c_off:c_off + SSD_STATE].astype(BF16)
        cb = lax.dot_general(cg, bg, nt_dims, preferred_element_type=F32)
        for pr in range(2):
            h0 = 4 * g + 2 * pr
            ps = slice(128 * (2 * g + pr), 128 * (2 * g + pr + 1))
            xs_pair = xact[:, ps]
            mats = []
            cols = []
            for hh in (h0, h0 + 1):
                colb = jnp.broadcast_to(a_cs[:, hh:hh + 1], (T, T))
                rowb = a_cs_t[hh:hh + 1, :]
                decay = jnp.exp(jnp.where(causal, colb - rowb, -jnp.inf))
                mats.append((cb * decay * dt_t[hh:hh + 1, :]).astype(BF16))
                cols.append(colb)
            x_lo = jnp.where(lo_lane, xs_pair, 0.0).astype(BF16)
            x_hi = jnp.where(lo_lane, 0.0, xs_pair).astype(BF16)
            y_diag = (jnp.dot(mats[0], x_lo, preferred_element_type=F32)
                      + jnp.dot(mats[1], x_hi, preferred_element_type=F32))
            st = state[ps, :]
            y_off = lax.dot_general(cg, st.astype(BF16), nt_dims, preferred_element_type=F32)
            y_off = y_off * jnp.exp(jnp.where(lo_lane, cols[0], cols[1]))
            ybuf[:, ps] = y_diag + y_off + xs_pair * dsk_ref[:, ps]

            last0 = a_cs_t[h0:h0 + 1, T - 1:T]
            last1 = a_cs_t[h0 + 1:h0 + 2, T - 1:T]
            w0 = jnp.exp(last0 - a_cs_t[h0:h0 + 1, :]) * dt_t[h0:h0 + 1, :]
            w1 = jnp.exp(last1 - a_cs_t[h0 + 1:h0 + 2, :]) * dt_t[h0 + 1:h0 + 2, :]
            wmat = jnp.where(lo_row, w0, w1)
            upd = jnp.dot((xs_pair.T * wmat).astype(BF16), bg, preferred_element_type=F32)
            cdec = jnp.where(lo_row, jnp.exp(last0), jnp.exp(last1))
            state[ps, :] = st * cdec + upd

    zz = z_ref[0, r0:r0 + T, :].astype(F32)
    gw = D_MODEL // SSD_GROUPS
    for g in range(SSD_GROUPS):
        gs = slice(g * gw, (g + 1) * gw)
        y = ybuf[:, gs] * _silu(zz[:, gs])
        ms = jnp.mean(y * y, axis=-1, keepdims=True)
        y_ref[0, r0:r0 + T, gs] = (y * lax.rsqrt(ms + NORM_EPS) * ng_ref[:, gs]).astype(y_ref.dtype)


def _ssd(proj3, dt3, cw, cb, dtb, alog, dsk, ng):
    bsz, n, _ = proj3.shape
    T = SSD_T
    TB = _pick(n, (3 * T, 2 * T, T))
    vec = lambda w: pl.BlockSpec((1, w), lambda b, t: (0, 0))
    shifts = _conv_shift_matrices()
    return pl.pallas_call(
        _ssd_kernel,
        grid=(bsz, n // TB),
        in_specs=[
            pl.BlockSpec((1, TB, D_MODEL), lambda b, t: (b, t, COL_XS)),
            pl.BlockSpec((1, TB, D_MODEL), lambda b, t: (b, t, COL_BC)),
            pl.BlockSpec((1, TB, D_MODEL), lambda b, t: (b, t, COL_Z)),
            pl.BlockSpec((1, TB, 128), lambda b, t: (b, t, 0)),
            pl.BlockSpec((SSD_CONV, 2 * D_MODEL), lambda b, t: (0, 0)),
            vec(2 * D_MODEL), vec(128),
            pl.BlockSpec((SSD_HEADS, T), lambda b, t: (0, 0)),
            vec(D_MODEL), vec(D_MODEL),
            pl.BlockSpec(shifts.shape, lambda b, t: (0, 0, 0)),
        ],
        out_specs=pl.BlockSpec((1, TB, D_MODEL), lambda b, t: (b, t, 0)),
        out_shape=jax.ShapeDtypeStruct((bsz, n, D_MODEL), BF16),
        scratch_shapes=[
            pltpu.VMEM((SSD_HALO, 2 * D_MODEL), BF16),
            pltpu.VMEM((T, 2 * D_MODEL), F32),
            pltpu.VMEM((T, D_MODEL), F32),
            pltpu.VMEM((SSD_HEADS * SSD_HEAD_DIM, SSD_STATE), F32),
        ],
        compiler_params=_cparams(2),
        name="ssd",
    )(proj3, proj3, proj3, dt3, cw, cb, dtb, alog, dsk, ng, shifts)


def _rope(t, c, s):
    lane = lax.broadcasted_iota(jnp.int32, t.shape, 1)
    first_half = (lane % ATTN_HEAD_DIM) < (ROT_DIM // 2)
    partner = jnp.where(first_half, pltpu.roll(t, 128 - ROT_DIM // 2, axis=1),
                        pltpu.roll(t, ROT_DIM // 2, axis=1))
    return t * c + partner * s


def _mask_features():
    r = np.arange(ATT_W)
    kf = np.zeros((ATT_W, 128), np.float32)
    kf[r, r // CHUNK] = 1.0
    qc = np.arange(ATT_T) // CHUNK
    qf = np.zeros((3, 128, ATT_T), np.float32)
    for c in range(ATT_W // CHUNK):
        qf[TYPE_EVEN, c, :] = np.where(c > qc, -MASK_BIG, 0.0)
        qf[TYPE_ODD, c, :] = np.where(c > qc + ATT_T // CHUNK, -MASK_BIG, 0.0)
    return jnp.asarray(kf, BF16), jnp.asarray(qf, BF16)


def _attn_schedule(nb):
    rows = []
    for b in range(nb):
        for w in range(b // 2 + 1):
            rows.append((b, w, TYPE_PLAIN if w < b // 2 else TYPE_EVEN + b % 2, 0))
    return np.asarray(rows, np.int32)


def _attn_kernel(desc_ref, q_ref, k_ref, v_ref, c_ref, s_ref, lp_ref, g_ref, kf_ref, qf_ref, o_ref,
                 qz, ks, km, vt, vtm, sa, sb, pa, pb, m_all, l_all, acc_all, *, seq, n_entries):
    A = ATT_T
    W = ATT_W
    RC = ATT_RC
    n = o_ref.shape[1]
    nb = seq // A
    lo_row = lax.broadcasted_iota(jnp.int32, (128, A), 0) < ATTN_HEAD_DIM
    qscale = ATTN_HEAD_DIM ** -0.5 * LOG2E

    def load(ref, rows):
        return ref[0, rows, :].astype(F32)

    def store_q(j, qf):
        qt = qf.T
        qz[0, j] = jnp.where(lo_row, qt, 0.0).astype(BF16)
        qz[1, j] = jnp.where(lo_row, 0.0, qt).astype(BF16)

    mrows = slice(0, N_META)
    cm, sm = c_ref[mrows, :], s_ref[mrows, :]
    km[...] = _rope(load(k_ref, mrows), cm, sm).astype(BF16)
    q_meta = _rope(load(q_ref, mrows), cm, sm) * qscale
    store_q(nb, jnp.concatenate([q_meta, jnp.zeros((A - N_META, 128), F32)], axis=0))
    v_meta = jnp.concatenate([load(v_ref, mrows), jnp.zeros((128 - N_META, 128), F32)], axis=0)
    vtm[...] = v_meta.T.astype(BF16)
    o_ref[0, N_META + seq:n, :] = jnp.zeros((n - N_META - seq, 128), o_ref.dtype)
    m_all[...] = jnp.full(m_all.shape, NEG_BIG, F32)
    l_all[...] = jnp.zeros(l_all.shape, F32)
    acc_all[...] = jnp.zeros(acc_all.shape, F32)
    if nb % 2:
        ks[nb * A:(nb + 1) * A, :] = jnp.zeros((A, 128), BF16)
        vt[nb] = jnp.zeros((128, A), BF16)

    def prep_block(j, carry):
        rows = pl.ds(pl.multiple_of(N_META + j * A, 16), A)
        c, s = c_ref[rows, :], s_ref[rows, :]
        store_q(j, _rope(load(q_ref, rows), c, s) * qscale)
        ks[pl.ds(pl.multiple_of(j * A, A), A), :] = _rope(load(k_ref, rows), c, s).astype(BF16)
        vt[j] = load(v_ref, rows).T.astype(BF16)
        return carry

    lax.fori_loop(0, nb, prep_block, 0)

    lp = lp_ref[...]
    lam_init = lp[4:5, 0:1]
    lam = (jnp.exp(jnp.sum(lp[0:1] * lp[1:2], axis=1, keepdims=True))
           - jnp.exp(jnp.sum(lp[2:3] * lp[3:4], axis=1, keepdims=True)) + lam_init)
    out_gain = g_ref[...] * (1.0 - lam_init)

    def step(t, par, mx, al):
        has_y = isinstance(t, jax.Array) or t < n_entries
        has_x = isinstance(t, jax.Array) or 1 <= t <= n_entries
        has_z = isinstance(t, jax.Array) or 2 <= t <= n_entries + 1
        s_y, s_x = (sa, sb) if par == 0 else (sb, sa)
        p_z, p_x = (pa, pb) if par == 0 else (pb, pa)
        new_mx, new_al = mx, al
        if has_z:
            qbz, wz = desc_ref[t - 2, 0], desc_ref[t - 2, 1]
            for m in range(2):
                pv = (jnp.dot(vt[2 * wz], p_z[m, 0:A, :], preferred_element_type=F32)
                      + jnp.dot(vt[2 * wz + 1], p_z[m, A:W, :], preferred_element_type=F32))
                acc_all[qbz, m] = al[m] * acc_all[qbz, m] + pv
        if has_x:
            qbx = desc_ref[t - 1, 0]
            m_old = [m_all[qbx, m] for m in range(2)]
            m_new = [jnp.maximum(m_old[m], mx[m]) for m in range(2)]
            new_al = tuple(jnp.exp2(m_old[m] - m_new[m]) for m in range(2))
            psum = [[], []]
        if has_y:
            qby, wy = desc_ref[t, 0], desc_ref[t, 1]
            feat = qf_ref[desc_ref[t, 2]]
            rhs = [jnp.concatenate([qz[m, qby], feat], axis=0) for m in range(2)]
            ymax = [[], []]
        for c in range(W // RC):
            rows = slice(c * RC, (c + 1) * RC)
            if has_y:
                krows = pl.ds(pl.multiple_of(wy * W + c * RC, 16), RC)
                kc = jnp.concatenate([ks[krows, :], kf_ref[rows, :]], axis=1)
            for m in range(2):
                if has_x:
                    p = jnp.exp2(s_x[m, rows, :] - m_new[m])
                    psum[m].append(jnp.sum(p, axis=0, keepdims=True))
                    p_x[m, rows, :] = p.astype(BF16)
                if has_y:
                    s = jnp.dot(kc, rhs[m], preferred_element_type=F32)
                    s_y[m, rows, :] = s
                    ymax[m].append(jnp.max(s, axis=0, keepdims=True))
        if has_x:
            for m in range(2):
                l_all[qbx, m] = new_al[m] * l_all[qbx, m] + functools.reduce(jnp.add, psum[m])
                m_all[qbx, m] = m_new[m]
        if has_y:
            new_mx = tuple(functools.reduce(jnp.maximum, ymax[m]) for m in range(2))
        return new_mx, new_al

    row0 = jnp.zeros((1, A), F32)
    mx, al = (row0, row0), (row0, row0)
    for t in (0, 1):
        mx, al = step(t, t % 2, mx, al)

    def group(u, carry):
        mx, al = carry[0:2], carry[2:4]
        for k in range(ATT_UNROLL):
            mx, al = step(2 + ATT_UNROLL * u + k, k % 2, mx, al)
        return mx + al

    n_mid = max(n_entries - 2, 0)
    carry = lax.fori_loop(0, n_mid // ATT_UNROLL, group, mx + al)
    mx, al = carry[0:2], carry[2:4]
    for t in range(2 + ATT_UNROLL * (n_mid // ATT_UNROLL), n_entries + 2):
        mx, al = step(t, t % 2, mx, al)

    kme = jnp.concatenate([km[...], jnp.zeros((N_META, 128), BF16)], axis=1)

    def finish(qb):
        outs = []
        for m in range(2):
            rhs = jnp.concatenate([qz[m, qb], qf_ref[TYPE_PLAIN]], axis=0)
            s = jnp.dot(kme, rhs, preferred_element_type=F32)
            m_old = m_all[qb, m]
            m_new = jnp.maximum(m_old, jnp.max(s, axis=0, keepdims=True))
            alpha = jnp.exp2(m_old - m_new)
            p = jnp.exp2(s - m_new)
            l = alpha * l_all[qb, m] + jnp.sum(p, axis=0, keepdims=True)
            pp = jnp.concatenate([p.astype(BF16), jnp.zeros((128 - N_META, A), BF16)], axis=0)
            acc = alpha * acc_all[qb, m] + jnp.dot(vtm[...], pp, preferred_element_type=F32)
            outs.append(acc * (1.0 / l))
        ot = (outs[0] - lam * outs[1]).T
        ms = jnp.mean(ot * ot, axis=-1, keepdims=True)
        res = (ot * lax.rsqrt(ms + SUBLN_EPS) * out_gain).astype(o_ref.dtype)

        if qb == nb:
            o_ref[0, 0:N_META, :] = res[0:N_META]
        else:
            o_ref[0, N_META + qb * A:N_META + (qb + 1) * A, :] = res

    for qb in range(nb + 1):
        finish(qb)


def _attention(proj3, rope_c, rope_s, lam_params, subln_g, seq):
    bsz, n, _ = proj3.shape
    A = ATT_T
    assert seq % A == 0 and n >= N_META + seq
    nb = seq // A
    hb = D_MODEL // 128
    nwin = -(-nb // 2)
    kfeat, qfeat = _mask_features()
    desc = _attn_schedule(nb)
    kern = functools.partial(_attn_kernel, seq=seq, n_entries=desc.shape[0])
    desc = jnp.asarray(desc)
    const = lambda shape: pl.BlockSpec(shape, lambda b, h: (0,) * len(shape))
    return pl.pallas_call(
        kern,
        grid=(bsz, ATTN_HEADS),
        in_specs=[
            pl.BlockSpec(memory_space=pltpu.SMEM),
            pl.BlockSpec((1, n, 128), lambda b, h: (b, 0, COL_Q * hb + h)),
            pl.BlockSpec((1, n, 128), lambda b, h: (b, 0, COL_K * hb + h)),
            pl.BlockSpec((1, n, 128), lambda b, h: (b, 0, COL_V * hb + h)),
            const((n, 128)), const((n, 128)), const((8, 128)), const((1, 128)),
            const((ATT_W, 128)), const((3, 128, A)),
        ],
        out_specs=pl.BlockSpec((1, n, 128), lambda b, h: (b, 0, h)),
        out_shape=jax.ShapeDtypeStruct((bsz, n, D_MODEL), BF16),
        scratch_shapes=[
            pltpu.VMEM((2, nb + 1, 128, A), BF16),
            pltpu.VMEM((nwin * ATT_W, 128), BF16),
            pltpu.VMEM((N_META, 128), BF16),
            pltpu.VMEM((2 * nwin, 128, A), BF16),
            pltpu.VMEM((128, 128), BF16),
            pltpu.VMEM((2, ATT_W, A), F32),
            pltpu.VMEM((2, ATT_W, A), F32),
            pltpu.VMEM((2, ATT_W, A), BF16),
            pltpu.VMEM((2, ATT_W, A), BF16),
            pltpu.VMEM((nb + 1, 2, 1, A), F32),
            pltpu.VMEM((nb + 1, 2, 1, A), F32),
            pltpu.VMEM((nb + 1, 2, 128, A), F32),
        ],
        compiler_params=_cparams(2),
        name="diff_attn",
    )(desc, proj3, proj3, proj3, rope_c, rope_s, lam_params, subln_g, kfeat, qfeat)


def _merge_kernel(h_ref, ys_ref, ya_ref, gs_ref, ga_ref, ws_ref, wa_ref, wo_ref, o_ref):
    a = jnp.dot(ys_ref[...], ws_ref[...], preferred_element_type=F32)
    b = jnp.dot(ya_ref[...], wa_ref[...], preferred_element_type=F32)
    m = _sigmoid(gs_ref[...].astype(F32)) * a + _sigmoid(ga_ref[...].astype(F32)) * b
    o_ref[...] = h_ref[...] + jnp.dot(m.astype(BF16), wo_ref[...], preferred_element_type=F32)


def _merge(h2d, ys2d, ya2d, proj2d, ws, wa, wo):
    m = h2d.shape[0]
    tm = _pick(m, (768, 512, 256, 128))
    row = lambda c: pl.BlockSpec((tm, D_MODEL), lambda i: (i, c))
    wspec = pl.BlockSpec((D_MODEL, D_MODEL), lambda i: (0, 0))
    return pl.pallas_call(
        _merge_kernel,
        grid=(m // tm,),
        in_specs=[row(0), row(0), row(0), row(COL_GS), row(COL_GA), wspec, wspec, wspec],
        out_specs=row(0),
        out_shape=jax.ShapeDtypeStruct((m, D_MODEL), F32),
        compiler_params=_cparams(1),
        name="merge",
    )(h2d, ys2d, ya2d, proj2d, proj2d, ws, wa, wo)


def _mlp_kernel(h_ref, g_ref, wg_ref, wv_ref, cwg_ref, cwv_ref, cbg_ref, cbv_ref, wd_ref, o_ref,
                u_ref, hbuf, carry):
    i = pl.program_id(1)
    j = pl.program_id(2)
    tm = u_ref.shape[0]

    @pl.when(j == 0)
    def _():
        x = h_ref[0]
        ms = jnp.mean(x * x, axis=-1, keepdims=True)
        u_ref[...] = (x * lax.rsqrt(ms + NORM_EPS) * g_ref[...]).astype(BF16)
        o_ref[0] = x

    u = u_ref[...]
    convs = []
    for idx, (w_ref, cw_ref, cb_ref) in enumerate(((wg_ref, cwg_ref, cbg_ref), (wv_ref, cwv_ref, cbv_ref))):
        hid = jnp.dot(u, w_ref[...], preferred_element_type=F32)
        prev = carry[j, idx]
        hbuf[idx, 0:8, :] = jnp.where(i == 0, jnp.zeros_like(prev), prev)
        hbuf[idx, 8:8 + tm, :] = hid
        conv = cb_ref[...] + cw_ref[2:3, :] * hbuf[idx, 8:8 + tm, :]
        conv = conv + cw_ref[1:2, :] * hbuf[idx, 7:7 + tm, :]
        conv = conv + cw_ref[0:1, :] * hbuf[idx, 6:6 + tm, :]
        carry[j, idx] = hbuf[idx, tm:tm + 8, :]
        convs.append(conv)
    act = (_silu(convs[0]) * convs[1]).astype(BF16)
    o_ref[0] += jnp.dot(act, wd_ref[...], preferred_element_type=F32)


def _mlp(h3, g, w_up, cw, cb, w_down):
    bsz, n, _ = h3.shape
    tm = _pick(n, (2112, 1056, 1024, 768, 512, 384, 256, 128))
    nj = D_FF // FF_T
    return pl.pallas_call(
        _mlp_kernel,
        grid=(bsz, n // tm, nj),
        in_specs=[
            pl.BlockSpec((1, tm, D_MODEL), lambda b, i, j: (b, i, 0)),
            pl.BlockSpec((1, D_MODEL), lambda b, i, j: (0, 0)),
            pl.BlockSpec((D_MODEL, FF_T), lambda b, i, j: (0, j)),
            pl.BlockSpec((D_MODEL, FF_T), lambda b, i, j: (0, j + nj)),
            pl.BlockSpec((MLP_CONV, FF_T), lambda b, i, j: (0, j)),
            pl.BlockSpec((MLP_CONV, FF_T), lambda b, i, j: (0, j + nj)),
            pl.BlockSpec((1, FF_T), lambda b, i, j: (0, j)),
            pl.BlockSpec((1, FF_T), lambda b, i, j: (0, j + nj)),
            pl.BlockSpec((FF_T, D_MODEL), lambda b, i, j: (j, 0)),
        ],
        out_specs=pl.BlockSpec((1, tm, D_MODEL), lambda b, i, j: (b, i, 0)),
        out_shape=jax.ShapeDtypeStruct((bsz, n, D_MODEL), F32),
        scratch_shapes=[
            pltpu.VMEM((tm, D_MODEL), BF16),
            pltpu.VMEM((2, tm + 8, FF_T), F32),
            pltpu.VMEM((nj, 2, 8, FF_T), F32),
        ],
        compiler_params=_cparams(3),
        name="convglu_mlp",
    )(h3, g, w_up, w_up, cw, cw, cb, cb, w_down)


def _final_kernel(a_ref, b_ref, g_ref, o_ref):
    tm = o_ref.shape[1]

    def norm(x):
        ms = jnp.mean(x * x, axis=-1, keepdims=True)
        return x * lax.rsqrt(ms + NORM_EPS) * g_ref[...]

    o_ref[0, 0:tm - N_META, :] = norm(a_ref[0, N_META:tm, :])
    o_ref[0, tm - N_META:tm, :] = norm(b_ref[0])


def _final_norm(h3, g, seq):
    bsz = h3.shape[0]
    tm = _pick(seq, (512, 256, 128, 64, 32))
    per = tm // N_META
    return pl.pallas_call(
        _final_kernel,
        grid=(bsz, seq // tm),
        in_specs=[
            pl.BlockSpec((1, tm, D_MODEL), lambda b, i: (b, i, 0)),
            pl.BlockSpec((1, N_META, D_MODEL), lambda b, i: (b, (i + 1) * per, 0)),
            pl.BlockSpec((1, D_MODEL), lambda b, i: (0, 0)),
        ],
        out_specs=pl.BlockSpec((1, tm, D_MODEL), lambda b, i: (b, i, 0)),
        out_shape=jax.ShapeDtypeStruct((bsz, seq, D_MODEL), F32),
        compiler_params=_cparams(2),
        name="final_norm",
    )(h3, h3, g)


def _rope_tables(n_pos):
    half = ROT_DIM // 2
    inv = 1.0 / (ROPE_THETA ** (jnp.arange(half, dtype=F32) * 2.0 / ROT_DIM))
    ang = jnp.arange(n_pos, dtype=F32)[:, None] * inv[None, :]
    cos, sin = jnp.cos(ang), jnp.sin(ang)
    rest = ATTN_HEAD_DIM - ROT_DIM
    c64 = jnp.concatenate([cos, cos, jnp.ones((n_pos, rest), F32)], axis=1)
    s64 = jnp.concatenate([-sin, sin, jnp.zeros((n_pos, rest), F32)], axis=1)
    return jnp.concatenate([c64, c64], axis=1), jnp.concatenate([s64, s64], axis=1)


def _pad_lanes(v, width=128):
    v = v.astype(F32).reshape(1, -1)
    return jnp.pad(v, ((0, 0), (0, width - v.shape[1])))


def kernel(x, meta_tokens, norm1_g, w_in, ssd_conv_w, ssd_conv_b, ssd_dt_bias, ssd_a_log, ssd_d, ssd_norm_g, lambda_q1, lambda_k1, lambda_q2, lambda_k2, attn_subln_g, w_ssd_branch, w_attn_branch, w_out, norm2_g, w_up, mlp_conv_w, mlp_conv_b, w_down, final_norm_g):
    bsz, seq, _ = x.shape
    depth = w_in.shape[0]
    n_tok = N_META + seq
    n = -(-n_tok // Q_BLOCK) * Q_BLOCK
    meta = jnp.broadcast_to(meta_tokens.astype(x.dtype)[None], (bsz, N_META, D_MODEL))
    h = jnp.concatenate([meta, x, jnp.zeros((bsz, n - n_tok, D_MODEL), x.dtype)], axis=1)
    rope_c, rope_s = _rope_tables(n)
    dt_lo = 3 * D_MODEL
    dt_hi = dt_lo + SSD_HEADS

    for l in range(depth):
        lam_init = 0.8 - 0.6 * math.exp(-0.3 * l)
        w_main = jnp.concatenate([w_in[l][:, :dt_lo], w_in[l][:, dt_hi:]], axis=1).astype(BF16)
        w_dt = jnp.pad(w_in[l][:, dt_lo:dt_hi], ((0, 0), (0, 128 - SSD_HEADS))).astype(BF16)
        proj, dt_raw = _inproj(h.reshape(bsz * n, D_MODEL), norm1_g[l].reshape(1, D_MODEL), w_main, w_dt)
        proj3 = proj.reshape(bsz, n, N_PROJ)

        y_ssd = _ssd(proj3, dt_raw.reshape(bsz, n, 128), ssd_conv_w[l], ssd_conv_b[l].reshape(1, -1),
                     _pad_lanes(ssd_dt_bias[l]),
                     jnp.broadcast_to(ssd_a_log[l].astype(F32)[:, None], (SSD_HEADS, SSD_T)),
                     jnp.repeat(ssd_d[l], SSD_HEAD_DIM).reshape(1, D_MODEL),
                     ssd_norm_g[l].reshape(1, D_MODEL))

        lam_params = jnp.concatenate([
            _pad_lanes(lambda_q1[l]), _pad_lanes(lambda_k1[l]), _pad_lanes(lambda_q2[l]),
            _pad_lanes(lambda_k2[l]), jnp.full((1, 128), lam_init, F32), jnp.zeros((3, 128), F32)], axis=0)
        y_attn = _attention(proj3, rope_c, rope_s, lam_params, attn_subln_g[l].reshape(1, 128), seq)

        h2d = _merge(h.reshape(bsz * n, D_MODEL), y_ssd.reshape(bsz * n, D_MODEL),
                     y_attn.reshape(bsz * n, D_MODEL), proj,
                     w_ssd_branch[l].astype(BF16), w_attn_branch[l].astype(BF16), w_out[l].astype(BF16))

        h = _mlp(h2d.reshape(bsz, n, D_MODEL), norm2_g[l].reshape(1, D_MODEL), w_up[l].astype(BF16),
                 mlp_conv_w[l], mlp_conv_b[l].reshape(1, -1), w_down[l].astype(BF16))

    return _final_norm(h, final_norm_g.reshape(1, D_MODEL), seq)
```

```python
import functools
import math

import numpy as np
import jax
import jax.numpy as jnp
from jax import lax
from jax.experimental import pallas as pl
from jax.experimental.pallas import tpu as pltpu

F32 = jnp.float32
BF16 = jnp.bfloat16

D_MODEL = 1024
N_META = 16
CHUNK = 64
Q_BLOCK = 128
NORM_EPS = 1e-6
SUBLN_EPS = 1e-5
SSD_HEADS = 16
SSD_HEAD_DIM = 64
SSD_GROUPS = 4
SSD_STATE = 128
SSD_CONV = 4
ATTN_HEADS = 8
ATTN_HEAD_DIM = 64
ROT_DIM = 16
ROPE_THETA = 500000.0
D_FF = 2816
MLP_CONV = 3

COL_Z, COL_XS, COL_BC, COL_Q, COL_K, COL_V, COL_GS, COL_GA = range(8)
N_PROJ = 8 * D_MODEL

SSD_T = 128
SSD_HALO = 16
FF_T = 256
VMEM_LIMIT = 56 * 1024 * 1024
NEG_BIG = -1e30
LOG2E = 1.4426950408889634

ATT_T = 256
ATT_W = 2 * ATT_T
ATT_RC = 512
ATT_UNROLL = 8
MASK_BIG = 2.0 ** 100
TYPE_PLAIN, TYPE_EVEN, TYPE_ODD = 0, 1, 2


def _cparams(n_axes):
    return pltpu.CompilerParams(dimension_semantics=("arbitrary",) * n_axes,
                                vmem_limit_bytes=VMEM_LIMIT)


def _pick(n, candidates):
    for c in candidates:
        if n % c == 0:
            return c
    raise ValueError(f"no block size for {n}")


def _sigmoid(x):
    return 1.0 / (1.0 + jnp.exp(-x))


def _silu(x):
    return x * _sigmoid(x)


def _inproj_kernel(x_ref, g_ref, w_ref, wdt_ref, o_ref, dt_ref, u_ref):
    @pl.when(pl.program_id(1) == 0)
    def _():
        x = x_ref[...]
        ms = jnp.mean(x * x, axis=-1, keepdims=True)
        u = (x * lax.rsqrt(ms + NORM_EPS) * g_ref[...]).astype(BF16)
        u_ref[...] = u
        dt_ref[...] = jnp.dot(u, wdt_ref[...], preferred_element_type=F32)

    o_ref[...] = jnp.dot(u_ref[...], w_ref[...], preferred_element_type=F32).astype(o_ref.dtype)


def _inproj(h2d, g, w, wdt):
    m = h2d.shape[0]
    tm = _pick(m, (1024, 512, 256, 128))
    tn = 4096
    return pl.pallas_call(
        _inproj_kernel,
        grid=(m // tm, N_PROJ // tn),
        in_specs=[
            pl.BlockSpec((tm, D_MODEL), lambda i, j: (i, 0)),
            pl.BlockSpec((1, D_MODEL), lambda i, j: (0, 0)),
            pl.BlockSpec((D_MODEL, tn), lambda i, j: (0, j)),
            pl.BlockSpec((D_MODEL, 128), lambda i, j: (0, 0)),
        ],
        out_specs=[
            pl.BlockSpec((tm, tn), lambda i, j: (i, j)),
            pl.BlockSpec((tm, 128), lambda i, j: (i, 0)),
        ],
        out_shape=[
            jax.ShapeDtypeStruct((m, N_PROJ), BF16),
            jax.ShapeDtypeStruct((m, 128), F32),
        ],
        scratch_shapes=[pltpu.VMEM((tm, D_MODEL), BF16)],
        compiler_params=_cparams(2),
        name="inproj",
    )(h2d, g, w, wdt)


def _conv_shift_matrices():
    t = np.arange(SSD_T)
    sh = np.zeros((SSD_CONV - 1, SSD_T, SSD_HALO + SSD_T), np.float32)
    for s in range(1, SSD_CONV):
        sh[s - 1, t, SSD_HALO + t - s] = 1.0
    return jnp.asarray(sh, BF16)


def _ssd_kernel(xs_ref, bc_ref, z_ref, dt_ref, cw_ref, cb_ref, dtb_ref, alog_ref, dsk_ref, ng_ref, sh_ref,
                y_ref, halo, xact, ybuf, state):
    @pl.when(pl.program_id(1) == 0)
    def _():
        halo[...] = jnp.zeros(halo.shape, BF16)
        state[...] = jnp.zeros(state.shape, F32)

    for sub in range(y_ref.shape[1] // SSD_T):
        _ssd_chunk(sub * SSD_T, xs_ref, bc_ref, z_ref, dt_ref, cw_ref, cb_ref, dtb_ref, alog_ref, dsk_ref,
                   ng_ref, sh_ref, y_ref, halo, xact, ybuf, state)


def _ssd_chunk(r0, xs_ref, bc_ref, z_ref, dt_ref, cw_ref, cb_ref, dtb_ref, alog_ref, dsk_ref, ng_ref, sh_ref,
               y_ref, halo, xact, ybuf, state):
    T = SSD_T
    W = 2 * D_MODEL

    CW = 512
    for c in range(W // CW):
        cs = slice(c * CW, (c + 1) * CW)
        src, off = (xs_ref, c * CW) if c * CW < D_MODEL else (bc_ref, c * CW - D_MODEL)
        cur = src[0, r0:r0 + T, off:off + CW]
        ext = jnp.concatenate([halo[:, cs], cur], axis=0)
        acc = cb_ref[:, cs] + cw_ref[SSD_CONV - 1:SSD_CONV, cs] * cur.astype(F32)
        for s in range(1, SSD_CONV):
            k = SSD_CONV - 1 - s
            acc = acc + cw_ref[k:k + 1, cs] * jnp.dot(sh_ref[s - 1], ext, preferred_element_type=F32)
        xact[:, cs] = _silu(acc)
        halo[:, cs] = cur[T - SSD_HALO:T, :]

    dtv = (dt_ref[0, r0:r0 + T, :] + dtb_ref[...]).T[0:SSD_HEADS, :]
    dt_t = jnp.maximum(dtv, 0.0) + jnp.log1p(jnp.exp(-jnp.abs(dtv)))
    adt_t = dt_t * (-jnp.exp(alog_ref[...]))
    row = lax.broadcasted_iota(jnp.int32, (T, T), 0)
    col = lax.broadcasted_iota(jnp.int32, (T, T), 1)
    causal = row >= col
    a_cs_t = jnp.dot(adt_t, (row <= col).astype(F32), preferred_element_type=F32,
                     precision=lax.Precision.HIGHEST)
    a_cs = jnp.concatenate([a_cs_t, jnp.zeros((T - SSD_HEADS, T), F32)], axis=0).T
    lo_lane = col < SSD_HEAD_DIM
    lo_row = row < SSD_HEAD_DIM

    nt_dims = (((1,), (1,)), ((), ()))
    for g in range(SSD_GROUPS):
        b_off = D_MODEL + SSD_STATE * g
        c_off = D_MODEL + SSD_GROUPS * SSD_STATE + SSD_STATE * g
        bg = xact[:, b_off:b_off + SSD_STATE].astype(BF16)
        cg = xact[:, c_off:c_off + SSD_STATE].astype(BF16)
        cb = lax.dot_general(cg, bg, nt_dims, preferred_element_type=F32)
        for pr in range(2):
            h0 = 4 * g + 2 * pr
            ps = slice(128 * (2 * g + pr), 128 * (2 * g + pr + 1))
            xs_pair = xact[:, ps]
            mats = []
            cols = []
            for hh in (h0, h0 + 1):
                colb = jnp.broadcast_to(a_cs[:, hh:hh + 1], (T, T))
                rowb = a_cs_t[hh:hh + 1, :]
                decay = jnp.exp(jnp.where(causal, colb - rowb, -jnp.inf))
                mats.append((cb * decay * dt_t[hh:hh + 1, :]).astype(BF16))
                cols.append(colb)
            x_lo = jnp.where(lo_lane, xs_pair, 0.0).astype(BF16)
            x_hi = jnp.where(lo_lane, 0.0, xs_pair).astype(BF16)
            y_diag = (jnp.dot(mats[0], x_lo, preferred_element_type=F32)
                      + jnp.dot(mats[1], x_hi, preferred_element_type=F32))
            st = state[ps, :]
            y_off = lax.dot_general(cg, st.astype(BF16), nt_dims, preferred_element_type=F32)
            y_off = y_off * jnp.exp(jnp.where(lo_lane, cols[0], cols[1]))
            ybuf[:, ps] = y_diag + y_off + xs_pair * dsk_ref[:, ps]

            last0 = a_cs_t[h0:h0 + 1, T - 1:T]
            last1 = a_cs_t[h0 + 1:h0 + 2, T - 1:T]
            w0 = jnp.exp(last0 - a_cs_t[h0:h0 + 1, :]) * dt_t[h0:h0 + 1, :]
            w1 = jnp.exp(last1 - a_cs_t[h0 + 1:h0 + 2, :]) * dt_t[h0 + 1:h0 + 2, :]
            wmat = jnp.where(lo_row, w0, w1)
            upd = jnp.dot((xs_pair.T * wmat).astype(BF16), bg, preferred_element_type=F32)
            cdec = jnp.where(lo_row, jnp.exp(last0), jnp.exp(last1))
            state[ps, :] = st * cdec + upd

    zz = z_ref[0, r0:r0 + T, :].astype(F32)
    gw = D_MODEL // SSD_GROUPS
    for g in range(SSD_GROUPS):
        gs = slice(g * gw, (g + 1) * gw)
        y = ybuf[:, gs] * _silu(zz[:, gs])
        ms = jnp.mean(y * y, axis=-1, keepdims=True)
        y_ref[0, r0:r0 + T, gs] = (y * lax.rsqrt(ms + NORM_EPS) * ng_ref[:, gs]).astype(y_ref.dtype)


def _ssd(proj3, dt3, cw, cb, dtb, alog, dsk, ng):
    bsz, n, _ = proj3.shape
    T = SSD_T
    TB = _pick(n, (3 * T, 2 * T, T))
    vec = lambda w: pl.BlockSpec((1, w), lambda b, t: (0, 0))
    shifts = _conv_shift_matrices()
    return pl.pallas_call(
        _ssd_kernel,
        grid=(bsz, n // TB),
        in_specs=[
            pl.BlockSpec((1, TB, D_MODEL), lambda b, t: (b, t, COL_XS)),
            pl.BlockSpec((1, TB, D_MODEL), lambda b, t: (b, t, COL_BC)),
            pl.BlockSpec((1, TB, D_MODEL), lambda b, t: (b, t, COL_Z)),
            pl.BlockSpec((1, TB, 128), lambda b, t: (b, t, 0)),
            pl.BlockSpec((SSD_CONV, 2 * D_MODEL), lambda b, t: (0, 0)),
            vec(2 * D_MODEL), vec(128),
            pl.BlockSpec((SSD_HEADS, T), lambda b, t: (0, 0)),
            vec(D_MODEL), vec(D_MODEL),
            pl.BlockSpec(shifts.shape, lambda b, t: (0, 0, 0)),
        ],
        out_specs=pl.BlockSpec((1, TB, D_MODEL), lambda b, t: (b, t, 0)),
        out_shape=jax.ShapeDtypeStruct((bsz, n, D_MODEL), BF16),
        scratch_shapes=[
            pltpu.VMEM((SSD_HALO, 2 * D_MODEL), BF16),
            pltpu.VMEM((T, 2 * D_MODEL), F32),
            pltpu.VMEM((T, D_MODEL), F32),
            pltpu.VMEM((SSD_HEADS * SSD_HEAD_DIM, SSD_STATE), F32),
        ],
        compiler_params=_cparams(2),
        name="ssd",
    )(proj3, proj3, proj3, dt3, cw, cb, dtb, alog, dsk, ng, shifts)


def _rope(t, c, s):
    lane = lax.broadcasted_iota(jnp.int32, t.shape, 1)
    first_half = (lane % ATTN_HEAD_DIM) < (ROT_DIM // 2)
    partner = jnp.where(first_half, pltpu.roll(t, 128 - ROT_DIM // 2, axis=1),
                        pltpu.roll(t, ROT_DIM // 2, axis=1))
    return t * c + partner * s


def _mask_features():
    r = np.arange(ATT_W)
    kf = np.zeros((ATT_W, 128), np.float32)
    kf[r, r // CHUNK] = 1.0
    qc = np.arange(ATT_T) // CHUNK
    qf = np.zeros((3, 128, ATT_T), np.float32)
    for c in range(ATT_W // CHUNK):
        qf[TYPE_EVEN, c, :] = np.where(c > qc, -MASK_BIG, 0.0)
        qf[TYPE_ODD, c, :] = np.where(c > qc + ATT_T // CHUNK, -MASK_BIG, 0.0)
    return jnp.asarray(kf, BF16), jnp.asarray(qf, BF16)


def _attn_schedule(nb):
    rows = []
    for b in range(nb):
        for w in range(b // 2 + 1):
            rows.append((b, w, TYPE_PLAIN if w < b // 2 else TYPE_EVEN + b % 2, 0))
    return np.asarray(rows, np.int32)


def _attn_kernel(desc_ref, q_ref, k_ref, v_ref, c_ref, s_ref, lp_ref, g_ref, kf_ref, qf_ref, o_ref,
                 qz, ks, km, vt, vtm, sa, sb, pa, pb, m_all, l_all, acc_all, *, seq, n_entries):
    A = ATT_T
    W = ATT_W
    RC = ATT_RC
    n = o_ref.shape[1]
    nb = seq // A
    lo_row = lax.broadcasted_iota(jnp.int32, (128, A), 0) < ATTN_HEAD_DIM
    qscale = ATTN_HEAD_DIM ** -0.5 * LOG2E

    def load(ref, rows):
        return ref[0, rows, :].astype(F32)

    def store_q(j, qf):
        qt = qf.T
        qz[0, j] = jnp.where(lo_row, qt, 0.0).astype(BF16)
        qz[1, j] = jnp.where(lo_row, 0.0, qt).astype(BF16)

    mrows = slice(0, N_META)
    cm, sm = c_ref[mrows, :], s_ref[mrows, :]
    km[...] = _rope(load(k_ref, mrows), cm, sm).astype(BF16)
    q_meta = _rope(load(q_ref, mrows), cm, sm) * qscale
    store_q(nb, jnp.concatenate([q_meta, jnp.zeros((A - N_META, 128), F32)], axis=0))
    v_meta = jnp.concatenate([load(v_ref, mrows), jnp.zeros((128 - N_META, 128), F32)], axis=0)
    vtm[...] = v_meta.T.astype(BF16)
    o_ref[0, N_META + seq:n, :] = jnp.zeros((n - N_META - seq, 128), o_ref.dtype)
    m_all[...] = jnp.full(m_all.shape, NEG_BIG, F32)
    l_all[...] = jnp.zeros(l_all.shape, F32)
    acc_all[...] = jnp.zeros(acc_all.shape, F32)
    if nb % 2:
        ks[nb * A:(nb + 1) * A, :] = jnp.zeros((A, 128), BF16)
        vt[nb] = jnp.zeros((128, A), BF16)

    def prep_block(j):
        rows = pl.ds(pl.multiple_of(N_META + j * A, 16), A)
        c, s = c_ref[rows, :], s_ref[rows, :]
        store_q(j, _rope(load(q_ref, rows), c, s) * qscale)
        ks[pl.ds(pl.multiple_of(j * A, A), A), :] = _rope(load(k_ref, rows), c, s).astype(BF16)
        vt[j] = load(v_ref, rows).T.astype(BF16)

    def prep_pair(u, carry):
        prep_block(2 * u)
        prep_block(2 * u + 1)
        return carry

    lax.fori_loop(0, nb // 2, prep_pair, 0)
    if nb % 2:
        prep_block(nb - 1)

    lp = lp_ref[...]
    lam_init = lp[4:5, 0:1]
    lam = (jnp.exp(jnp.sum(lp[0:1] * lp[1:2], axis=1, keepdims=True))
           - jnp.exp(jnp.sum(lp[2:3] * lp[3:4], axis=1, keepdims=True)) + lam_init)
    out_gain = g_ref[...] * (1.0 - lam_init)

    def step(t, par, mx, al):
        has_y = isinstance(t, jax.Array) or t < n_entries
        has_x = isinstance(t, jax.Array) or 1 <= t <= n_entries
        has_z = isinstance(t, jax.Array) or 2 <= t <= n_entries + 1
        s_y, s_x = (sa, sb) if par == 0 else (sb, sa)
        p_z, p_x = (pa, pb) if par == 0 else (pb, pa)
        new_mx, new_al = mx, al
        if has_z:
            qbz, wz = desc_ref[t - 2, 0], desc_ref[t - 2, 1]
            for m in range(2):
                pv = (jnp.dot(vt[2 * wz], p_z[m, 0:A, :], preferred_element_type=F32)
                      + jnp.dot(vt[2 * wz + 1], p_z[m, A:W, :], preferred_element_type=F32))
                acc_all[qbz, m] = al[m] * acc_all[qbz, m] + pv
        if has_x:
            qbx = desc_ref[t - 1, 0]
            m_old = [m_all[qbx, m] for m in range(2)]
            m_new = [jnp.maximum(m_old[m], mx[m]) for m in range(2)]
            new_al = tuple(jnp.exp2(m_old[m] - m_new[m]) for m in range(2))
            psum = [[], []]
        if has_y:
            qby, wy = desc_ref[t, 0], desc_ref[t, 1]
            feat = qf_ref[desc_ref[t, 2]]
            rhs = [jnp.concatenate([qz[m, qby], feat], axis=0) for m in range(2)]
            ymax = [[], []]
        for c in range(W // RC):
            rows = slice(c * RC, (c + 1) * RC)
            if has_y:
                krows = pl.ds(pl.multiple_of(wy * W + c * RC, 16), RC)
                kc = jnp.concatenate([ks[krows, :], kf_ref[rows, :]], axis=1)
            for m in range(2):
                if has_x:
                    p = jnp.exp2(s_x[m, rows, :] - m_new[m])
                    psum[m].append(jnp.sum(p, axis=0, keepdims=True))
                    p_x[m, rows, :] = p.astype(BF16)
                if has_y:
                    s = jnp.dot(kc, rhs[m], preferred_element_type=F32)
                    s_y[m, rows, :] = s
                    ymax[m].append(jnp.max(s, axis=0, keepdims=True))
        if has_x:
            for m in range(2):
                l_all[qbx, m] = new_al[m] * l_all[qbx, m] + functools.reduce(jnp.add, psum[m])
                m_all[qbx, m] = m_new[m]
        if has_y:
            new_mx = tuple(functools.reduce(jnp.maximum, ymax[m]) for m in range(2))
        return new_mx, new_al

    row0 = jnp.zeros((1, A), F32)
    mx, al = (row0, row0), (row0, row0)
    for t in (0, 1):
        mx, al = step(t, t % 2, mx, al)

    def group(u, carry):
        mx, al = carry[0:2], carry[2:4]
        for k in range(ATT_UNROLL):
            mx, al = step(2 + ATT_UNROLL * u + k, k % 2, mx, al)
        return mx + al

    n_mid = max(n_entries - 2, 0)
    carry = lax.fori_loop(0, n_mid // ATT_UNROLL, group, mx + al)
    mx, al = carry[0:2], carry[2:4]
    for t in range(2 + ATT_UNROLL * (n_mid // ATT_UNROLL), n_entries + 2):
        mx, al = step(t, t % 2, mx, al)

    kme = jnp.concatenate([km[...], jnp.zeros((N_META, 128), BF16)], axis=1)

    def finish(qb):
        outs = []
        for m in range(2):
            rhs = jnp.concatenate([qz[m, qb], qf_ref[TYPE_PLAIN]], axis=0)
            s = jnp.dot(kme, rhs, preferred_element_type=F32)
            m_old = m_all[qb, m]
            m_new = jnp.maximum(m_old, jnp.max(s, axis=0, keepdims=True))
            alpha = jnp.exp2(m_old - m_new)
            p = jnp.exp2(s - m_new)
            l = alpha * l_all[qb, m] + jnp.sum(p, axis=0, keepdims=True)
            pp = jnp.concatenate([p.astype(BF16), jnp.zeros((128 - N_META, A), BF16)], axis=0)
            acc = alpha * acc_all[qb, m] + jnp.dot(vtm[...], pp, preferred_element_type=F32)
            outs.append(acc * (1.0 / l))
        ot = (outs[0] - lam * outs[1]).T
        ms = jnp.mean(ot * ot, axis=-1, keepdims=True)
        res = (ot * lax.rsqrt(ms + SUBLN_EPS) * out_gain).astype(o_ref.dtype)

        if qb == nb:
            o_ref[0, 0:N_META, :] = res[0:N_META]
        else:
            o_ref[0, N_META + qb * A:N_META + (qb + 1) * A, :] = res

    for qb in range(nb + 1):
        finish(qb)


def _attention(proj3, rope_c, rope_s, lam_params, subln_g, seq):
    bsz, n, _ = proj3.shape
    A = ATT_T
    assert seq % A == 0 and n >= N_META + seq
    nb = seq // A
    hb = D_MODEL // 128
    nwin = -(-nb // 2)
    kfeat, qfeat = _mask_features()
    desc = _attn_schedule(nb)
    kern = functools.partial(_attn_kernel, seq=seq, n_entries=desc.shape[0])
    desc = jnp.asarray(desc)
    const = lambda shape: pl.BlockSpec(shape, lambda b, h: (0,) * len(shape))
    return pl.pallas_call(
        kern,
        grid=(bsz, ATTN_HEADS),
        in_specs=[
            pl.BlockSpec(memory_space=pltpu.SMEM),
            pl.BlockSpec((1, n, 128), lambda b, h: (b, 0, COL_Q * hb + h)),
            pl.BlockSpec((1, n, 128), lambda b, h: (b, 0, COL_K * hb + h)),
            pl.BlockSpec((1, n, 128), lambda b, h: (b, 0, COL_V * hb + h)),
            const((n, 128)), const((n, 128)), const((8, 128)), const((1, 128)),
            const((ATT_W, 128)), const((3, 128, A)),
        ],
        out_specs=pl.BlockSpec((1, n, 128), lambda b, h: (b, 0, h)),
        out_shape=jax.ShapeDtypeStruct((bsz, n, D_MODEL), BF16),
        scratch_shapes=[
            pltpu.VMEM((2, nb + 1, 128, A), BF16),
            pltpu.VMEM((nwin * ATT_W, 128), BF16),
            pltpu.VMEM((N_META, 128), BF16),
            pltpu.VMEM((2 * nwin, 128, A), BF16),
            pltpu.VMEM((128, 128), BF16),
            pltpu.VMEM((2, ATT_W, A), F32),
            pltpu.VMEM((2, ATT_W, A), F32),
            pltpu.VMEM((2, ATT_W, A), BF16),
            pltpu.VMEM((2, ATT_W, A), BF16),
            pltpu.VMEM((nb + 1, 2, 1, A), F32),
            pltpu.VMEM((nb + 1, 2, 1, A), F32),
            pltpu.VMEM((nb + 1, 2, 128, A), F32),
        ],
        compiler_params=_cparams(2),
        name="diff_attn",
    )(desc, proj3, proj3, proj3, rope_c, rope_s, lam_params, subln_g, kfeat, qfeat)


def _merge_kernel(h_ref, ys_ref, ya_ref, gs_ref, ga_ref, ws_ref, wa_ref, wo_ref, o_ref):
    a = jnp.dot(ys_ref[...], ws_ref[...], preferred_element_type=F32)
    b = jnp.dot(ya_ref[...], wa_ref[...], preferred_element_type=F32)
    m = _sigmoid(gs_ref[...].astype(F32)) * a + _sigmoid(ga_ref[...].astype(F32)) * b
    o_ref[...] = h_ref[...] + jnp.dot(m.astype(BF16), wo_ref[...], preferred_element_type=F32)


def _merge(h2d, ys2d, ya2d, proj2d, ws, wa, wo):
    m = h2d.shape[0]
    tm = _pick(m, (768, 512, 256, 128))
    row = lambda c: pl.BlockSpec((tm, D_MODEL), lambda i: (i, c))
    wspec = pl.BlockSpec((D_MODEL, D_MODEL), lambda i: (0, 0))
    return pl.pallas_call(
        _merge_kernel,
        grid=(m // tm,),
        in_specs=[row(0), row(0), row(0), row(COL_GS), row(COL_GA), wspec, wspec, wspec],
        out_specs=row(0),
        out_shape=jax.ShapeDtypeStruct((m, D_MODEL), F32),
        compiler_params=_cparams(1),
        name="merge",
    )(h2d, ys2d, ya2d, proj2d, proj2d, ws, wa, wo)


def _mlp_kernel(h_ref, g_ref, wg_ref, wv_ref, cwg_ref, cwv_ref, cbg_ref, cbv_ref, wd_ref, o_ref,
                u_ref, hbuf, carry):
    i = pl.program_id(1)
    j = pl.program_id(2)
    tm = u_ref.shape[0]

    @pl.when(j == 0)
    def _():
        x = h_ref[0]
        ms = jnp.mean(x * x, axis=-1, keepdims=True)
        u_ref[...] = (x * lax.rsqrt(ms + NORM_EPS) * g_ref[...]).astype(BF16)
        o_ref[0] = x

    u = u_ref[...]
    convs = []
    for idx, (w_ref, cw_ref, cb_ref) in enumerate(((wg_ref, cwg_ref, cbg_ref), (wv_ref, cwv_ref, cbv_ref))):
        hid = jnp.dot(u, w_ref[...], preferred_element_type=F32)
        prev = carry[j, idx]
        hbuf[idx, 0:8, :] = jnp.where(i == 0, jnp.zeros_like(prev), prev)
        hbuf[idx, 8:8 + tm, :] = hid
        conv = cb_ref[...] + cw_ref[2:3, :] * hbuf[idx, 8:8 + tm, :]
        conv = conv + cw_ref[1:2, :] * hbuf[idx, 7:7 + tm, :]
        conv = conv + cw_ref[0:1, :] * hbuf[idx, 6:6 + tm, :]
        carry[j, idx] = hbuf[idx, tm:tm + 8, :]
        convs.append(conv)
    act = (_silu(convs[0]) * convs[1]).astype(BF16)
    o_ref[0] += jnp.dot(act, wd_ref[...], preferred_element_type=F32)


def _mlp(h3, g, w_up, cw, cb, w_down):
    bsz, n, _ = h3.shape
    tm = _pick(n, (2112, 1056, 1024, 768, 512, 384, 256, 128))
    nj = D_FF // FF_T
    return pl.pallas_call(
        _mlp_kernel,
        grid=(bsz, n // tm, nj),
        in_specs=[
            pl.BlockSpec((1, tm, D_MODEL), lambda b, i, j: (b, i, 0)),
            pl.BlockSpec((1, D_MODEL), lambda b, i, j: (0, 0)),
            pl.BlockSpec((D_MODEL, FF_T), lambda b, i, j: (0, j)),
            pl.BlockSpec((D_MODEL, FF_T), lambda b, i, j: (0, j + nj)),
            pl.BlockSpec((MLP_CONV, FF_T), lambda b, i, j: (0, j)),
            pl.BlockSpec((MLP_CONV, FF_T), lambda b, i, j: (0, j + nj)),
            pl.BlockSpec((1, FF_T), lambda b, i, j: (0, j)),
            pl.BlockSpec((1, FF_T), lambda b, i, j: (0, j + nj)),
            pl.BlockSpec((FF_T, D_MODEL), lambda b, i, j: (j, 0)),
        ],
        out_specs=pl.BlockSpec((1, tm, D_MODEL), lambda b, i, j: (b, i, 0)),
        out_shape=jax.ShapeDtypeStruct((bsz, n, D_MODEL), F32),
        scratch_shapes=[
            pltpu.VMEM((tm, D_MODEL), BF16),
            pltpu.VMEM((2, tm + 8, FF_T), F32),
            pltpu.VMEM((nj, 2, 8, FF_T), F32),
        ],
        compiler_params=_cparams(3),
        name="convglu_mlp",
    )(h3, g, w_up, w_up, cw, cw, cb, cb, w_down)


def _final_kernel(a_ref, b_ref, g_ref, o_ref):
    tm = o_ref.shape[1]

    def norm(x):
        ms = jnp.mean(x * x, axis=-1, keepdims=True)
        return x * lax.rsqrt(ms + NORM_EPS) * g_ref[...]

    o_ref[0, 0:tm - N_META, :] = norm(a_ref[0, N_META:tm, :])
    o_ref[0, tm - N_META:tm, :] = norm(b_ref[0])


def _final_norm(h3, g, seq):
    bsz = h3.shape[0]
    tm = _pick(seq, (512, 256, 128, 64, 32))
    per = tm // N_META
    return pl.pallas_call(
        _final_kernel,
        grid=(bsz, seq // tm),
        in_specs=[
            pl.BlockSpec((1, tm, D_MODEL), lambda b, i: (b, i, 0)),
            pl.BlockSpec((1, N_META, D_MODEL), lambda b, i: (b, (i + 1) * per, 0)),
            pl.BlockSpec((1, D_MODEL), lambda b, i: (0, 0)),
        ],
        out_specs=pl.BlockSpec((1, tm, D_MODEL), lambda b, i: (b, i, 0)),
        out_shape=jax.ShapeDtypeStruct((bsz, seq, D_MODEL), F32),
        compiler_params=_cparams(2),
        name="final_norm",
    )(h3, h3, g)


def _rope_tables(n_pos):
    half = ROT_DIM // 2
    inv = 1.0 / (ROPE_THETA ** (jnp.arange(half, dtype=F32) * 2.0 / ROT_DIM))
    ang = jnp.arange(n_pos, dtype=F32)[:, None] * inv[None, :]
    cos, sin = jnp.cos(ang), jnp.sin(ang)
    rest = ATTN_HEAD_DIM - ROT_DIM
    c64 = jnp.concatenate([cos, cos, jnp.ones((n_pos, rest), F32)], axis=1)
    s64 = jnp.concatenate([-sin, sin, jnp.zeros((n_pos, rest), F32)], axis=1)
    return jnp.concatenate([c64, c64], axis=1), jnp.concatenate([s64, s64], axis=1)


def _pad_lanes(v, width=128):
    v = v.astype(F32).reshape(1, -1)
    return jnp.pad(v, ((0, 0), (0, width - v.shape[1])))


def kernel(x, meta_tokens, norm1_g, w_in, ssd_conv_w, ssd_conv_b, ssd_dt_bias, ssd_a_log, ssd_d, ssd_norm_g, lambda_q1, lambda_k1, lambda_q2, lambda_k2, attn_subln_g, w_ssd_branch, w_attn_branch, w_out, norm2_g, w_up, mlp_conv_w, mlp_conv_b, w_down, final_norm_g):
    bsz, seq, _ = x.shape
    depth = w_in.shape[0]
    n_tok = N_META + seq
    n = -(-n_tok // Q_BLOCK) * Q_BLOCK
    meta = jnp.broadcast_to(meta_tokens.astype(x.dtype)[None], (bsz, N_META, D_MODEL))
    h = jnp.concatenate([meta, x, jnp.zeros((bsz, n - n_tok, D_MODEL), x.dtype)], axis=1)
    rope_c, rope_s = _rope_tables(n)
    dt_lo = 3 * D_MODEL
    dt_hi = dt_lo + SSD_HEADS

    for l in range(depth):
        lam_init = 0.8 - 0.6 * math.exp(-0.3 * l)
        w_main = jnp.concatenate([w_in[l][:, :dt_lo], w_in[l][:, dt_hi:]], axis=1).astype(BF16)
        w_dt = jnp.pad(w_in[l][:, dt_lo:dt_hi], ((0, 0), (0, 128 - SSD_HEADS))).astype(BF16)
        proj, dt_raw = _inproj(h.reshape(bsz * n, D_MODEL), norm1_g[l].reshape(1, D_MODEL), w_main, w_dt)
        proj3 = proj.reshape(bsz, n, N_PROJ)

        y_ssd = _ssd(proj3, dt_raw.reshape(bsz, n, 128), ssd_conv_w[l], ssd_conv_b[l].reshape(1, -1),
                     _pad_lanes(ssd_dt_bias[l]),
                     jnp.broadcast_to(ssd_a_log[l].astype(F32)[:, None], (SSD_HEADS, SSD_T)),
                     jnp.repeat(ssd_d[l], SSD_HEAD_DIM).reshape(1, D_MODEL),
                     ssd_norm_g[l].reshape(1, D_MODEL))

        lam_params = jnp.concatenate([
            _pad_lanes(lambda_q1[l]), _pad_lanes(lambda_k1[l]), _pad_lanes(lambda_q2[l]),
            _pad_lanes(lambda_k2[l]), jnp.full((1, 128), lam_init, F32), jnp.zeros((3, 128), F32)], axis=0)
        y_attn = _attention(proj3, rope_c, rope_s, lam_params, attn_subln_g[l].reshape(1, 128), seq)

        h2d = _merge(h.reshape(bsz * n, D_MODEL), y_ssd.reshape(bsz * n, D_MODEL),
                     y_attn.reshape(bsz * n, D_MODEL), proj,
                     w_ssd_branch[l].astype(BF16), w_attn_branch[l].astype(BF16), w_out[l].astype(BF16))

        h = _mlp(h2d.reshape(bsz, n, D_MODEL), norm2_g[l].reshape(1, D_MODEL), w_up[l].astype(BF16),
                 mlp_conv_w[l], mlp_conv_b[l].reshape(1, -1), w_down[l].astype(BF16))

    return _final_norm(h, final_norm_g.reshape(1, D_MODEL), seq)
```

```python
import functools
import math

import numpy as np
import jax
import jax.numpy as jnp
from jax import lax
from jax.experimental import pallas as pl
from jax.experimental.pallas import tpu as pltpu

F32 = jnp.float32
BF16 = jnp.bfloat16

D_MODEL = 1024
N_META = 16
CHUNK = 64
Q_BLOCK = 128
NORM_EPS = 1e-6
SUBLN_EPS = 1e-5
SSD_HEADS = 16
SSD_HEAD_DIM = 64
SSD_GROUPS = 4
SSD_STATE = 128
SSD_CONV = 4
ATTN_HEADS = 8
ATTN_HEAD_DIM = 64
ROT_DIM = 16
ROPE_THETA = 500000.0
D_FF = 2816
MLP_CONV = 3

COL_Z, COL_XS, COL_BC, COL_Q, COL_K, COL_V, COL_GS, COL_GA = range(8)
N_PROJ = 8 * D_MODEL

SSD_T = 128
SSD_HALO = 16
FF_T = 256
VMEM_LIMIT = 56 * 1024 * 1024
NEG_BIG = -1e30
LOG2E = 1.4426950408889634

ATT_T = 256
ATT_W = 2 * ATT_T
ATT_RC = 512
ATT_UNROLL = 10
ATT_PREP = 4
MASK_BIG = 2.0 ** 100
TYPE_PLAIN, TYPE_EVEN, TYPE_ODD = 0, 1, 2


def _cparams(n_axes):
    return pltpu.CompilerParams(dimension_semantics=("arbitrary",) * n_axes,
                                vmem_limit_bytes=VMEM_LIMIT)


def _pick(n, candidates):
    for c in candidates:
        if n % c == 0:
            return c
    raise ValueError(f"no block size for {n}")


def _sigmoid(x):
    return 1.0 / (1.0 + jnp.exp(-x))


def _silu(x):
    return x * _sigmoid(x)


def _inproj_kernel(x_ref, g_ref, w_ref, wdt_ref, o_ref, dt_ref, u_ref):
    @pl.when(pl.program_id(1) == 0)
    def _():
        x = x_ref[...]
        ms = jnp.mean(x * x, axis=-1, keepdims=True)
        u = (x * lax.rsqrt(ms + NORM_EPS) * g_ref[...]).astype(BF16)
        u_ref[...] = u
        dt_ref[...] = jnp.dot(u, wdt_ref[...], preferred_element_type=F32)

    o_ref[...] = jnp.dot(u_ref[...], w_ref[...], preferred_element_type=F32).astype(o_ref.dtype)


def _inproj(h2d, g, w, wdt):
    m = h2d.shape[0]
    tm = _pick(m, (1024, 512, 256, 128))
    tn = 4096
    return pl.pallas_call(
        _inproj_kernel,
        grid=(m // tm, N_PROJ // tn),
        in_specs=[
            pl.BlockSpec((tm, D_MODEL), lambda i, j: (i, 0)),
            pl.BlockSpec((1, D_MODEL), lambda i, j: (0, 0)),
            pl.BlockSpec((D_MODEL, tn), lambda i, j: (0, j)),
            pl.BlockSpec((D_MODEL, 128), lambda i, j: (0, 0)),
        ],
        out_specs=[
            pl.BlockSpec((tm, tn), lambda i, j: (i, j)),
            pl.BlockSpec((tm, 128), lambda i, j: (i, 0)),
        ],
        out_shape=[
            jax.ShapeDtypeStruct((m, N_PROJ), BF16),
            jax.ShapeDtypeStruct((m, 128), F32),
        ],
        scratch_shapes=[pltpu.VMEM((tm, D_MODEL), BF16)],
        compiler_params=_cparams(2),
        name="inproj",
    )(h2d, g, w, wdt)


def _conv_shift_matrices():
    t = np.arange(SSD_T)
    sh = np.zeros((SSD_CONV - 1, SSD_T, SSD_HALO + SSD_T), np.float32)
    for s in range(1, SSD_CONV):
        sh[s - 1, t, SSD_HALO + t - s] = 1.0
    return jnp.asarray(sh, BF16)


def _ssd_kernel(xs_ref, bc_ref, z_ref, dt_ref, cw_ref, cb_ref, dtb_ref, alog_ref, dsk_ref, ng_ref, sh_ref,
                y_ref, halo, xact, ybuf, state):
    @pl.when(pl.program_id(1) == 0)
    def _():
        halo[...] = jnp.zeros(halo.shape, BF16)
        state[...] = jnp.zeros(state.shape, F32)

    for sub in range(y_ref.shape[1] // SSD_T):
        _ssd_chunk(sub * SSD_T, xs_ref, bc_ref, z_ref, dt_ref, cw_ref, cb_ref, dtb_ref, alog_ref, dsk_ref,
                   ng_ref, sh_ref, y_ref, halo, xact, ybuf, state)


def _ssd_chunk(r0, xs_ref, bc_ref, z_ref, dt_ref, cw_ref, cb_ref, dtb_ref, alog_ref, dsk_ref, ng_ref, sh_ref,
               y_ref, halo, xact, ybuf, state):
    T = SSD_T
    W = 2 * D_MODEL

    CW = 512
    for c in range(W // CW):
        cs = slice(c * CW, (c + 1) * CW)
        src, off = (xs_ref, c * CW) if c * CW < D_MODEL else (bc_ref, c * CW - D_MODEL)
        cur = src[0, r0:r0 + T, off:off + CW]
        ext = jnp.concatenate([halo[:, cs], cur], axis=0)
        acc = cb_ref[:, cs] + cw_ref[SSD_CONV - 1:SSD_CONV, cs] * cur.astype(F32)
        for s in range(1, SSD_CONV):
            k = SSD_CONV - 1 - s
            acc = acc + cw_ref[k:k + 1, cs] * jnp.dot(sh_ref[s - 1], ext, preferred_element_type=F32)
        xact[:, cs] = _silu(acc)
        halo[:, cs] = cur[T - SSD_HALO:T, :]

    dtv = (dt_ref[0, r0:r0 + T, :] + dtb_ref[...]).T[0:SSD_HEADS, :]
    dt_t = jnp.maximum(dtv, 0.0) + jnp.log1p(jnp.exp(-jnp.abs(dtv)))
    adt_t = dt_t * (-jnp.exp(alog_ref[...]))
    row = lax.broadcasted_iota(jnp.int32, (T, T), 0)
    col = lax.broadcasted_iota(jnp.int32, (T, T), 1)
    causal = row >= col
    a_cs_t = jnp.dot(adt_t, (row <= col).astype(F32), preferred_element_type=F32,
                     precision=lax.Precision.HIGHEST)
    a_cs = jnp.concatenate([a_cs_t, jnp.zeros((T - SSD_HEADS, T), F32)], axis=0).T
    lo_lane = col < SSD_HEAD_DIM
    lo_row = row < SSD_HEAD_DIM

    nt_dims = (((1,), (1,)), ((), ()))
    for g in range(SSD_GROUPS):
        b_off = D_MODEL + SSD_STATE * g
        c_off = D_MODEL + SSD_GROUPS * SSD_STATE + SSD_STATE * g
        bg = xact[:, b_off:b_off + SSD_STATE].astype(BF16)
        cg = xact[:, c_off:c_off + SSD_STATE].astype(BF16)
        cb = lax.dot_general(cg, bg, nt_dims, preferred_element_type=F32)
        for pr in range(2):
            h0 = 4 * g + 2 * pr
            ps = slice(128 * (2 * g + pr), 128 * (2 * g + pr + 1))
            xs_pair = xact[:, ps]
            mats = []
            cols = []
            for hh in (h0, h0 + 1):
                colb = jnp.broadcast_to(a_cs[:, hh:hh + 1], (T, T))
                rowb = a_cs_t[hh:hh + 1, :]
                decay = jnp.exp(jnp.where(causal, colb - rowb, -jnp.inf))
                mats.append((cb * decay * dt_t[hh:hh + 1, :]).astype(BF16))
                cols.append(colb)
            x_lo = jnp.where(lo_lane, xs_pair, 0.0).astype(BF16)
            x_hi = jnp.where(lo_lane, 0.0, xs_pair).astype(BF16)
            y_diag = (jnp.dot(mats[0], x_lo, preferred_element_type=F32)
                      + jnp.dot(mats[1], x_hi, preferred_element_type=F32))
            st = state[ps, :]
            y_off = lax.dot_general(cg, st.astype(BF16), nt_dims, preferred_element_type=F32)
            y_off = y_off * jnp.exp(jnp.where(lo_lane, cols[0], cols[1]))
            ybuf[:, ps] = y_diag + y_off + xs_pair * dsk_ref[:, ps]

            last0 = a_cs_t[h0:h0 + 1, T - 1:T]
            last1 = a_cs_t[h0 + 1:h0 + 2, T - 1:T]
            w0 = jnp.exp(last0 - a_cs_t[h0:h0 + 1, :]) * dt_t[h0:h0 + 1, :]
            w1 = jnp.exp(last1 - a_cs_t[h0 + 1:h0 + 2, :]) * dt_t[h0 + 1:h0 + 2, :]
            wmat = jnp.where(lo_row, w0, w1)
            upd = jnp.dot((xs_pair.T * wmat).astype(BF16), bg, preferred_element_type=F32)
            cdec = jnp.where(lo_row, jnp.exp(last0), jnp.exp(last1))
            state[ps, :] = st * cdec + upd

    zz = z_ref[0, r0:r0 + T, :].astype(F32)
    gw = D_MODEL // SSD_GROUPS
    for g in range(SSD_GROUPS):
        gs = slice(g * gw, (g + 1) * gw)
        y = ybuf[:, gs] * _silu(zz[:, gs])
        ms = jnp.mean(y * y, axis=-1, keepdims=True)
        y_ref[0, r0:r0 + T, gs] = (y * lax.rsqrt(ms + NORM_EPS) * ng_ref[:, gs]).astype(y_ref.dtype)


def _ssd(proj3, dt3, cw, cb, dtb, alog, dsk, ng):
    bsz, n, _ = proj3.shape
    T = SSD_T
    TB = _pick(n, (3 * T, 2 * T, T))
    vec = lambda w: pl.BlockSpec((1, w), lambda b, t: (0, 0))
    shifts = _conv_shift_matrices()
    return pl.pallas_call(
        _ssd_kernel,
        grid=(bsz, n // TB),
        in_specs=[
            pl.BlockSpec((1, TB, D_MODEL), lambda b, t: (b, t, COL_XS)),
            pl.BlockSpec((1, TB, D_MODEL), lambda b, t: (b, t, COL_BC)),
            pl.BlockSpec((1, TB, D_MODEL), lambda b, t: (b, t, COL_Z)),
            pl.BlockSpec((1, TB, 128), lambda b, t: (b, t, 0)),
            pl.BlockSpec((SSD_CONV, 2 * D_MODEL), lambda b, t: (0, 0)),
            vec(2 * D_MODEL), vec(128),
            pl.BlockSpec((SSD_HEADS, T), lambda b, t: (0, 0)),
            vec(D_MODEL), vec(D_MODEL),
            pl.BlockSpec(shifts.shape, lambda b, t: (0, 0, 0)),
        ],
        out_specs=pl.BlockSpec((1, TB, D_MODEL), lambda b, t: (b, t, 0)),
        out_shape=jax.ShapeDtypeStruct((bsz, n, D_MODEL), BF16),
        scratch_shapes=[
            pltpu.VMEM((SSD_HALO, 2 * D_MODEL), BF16),
            pltpu.VMEM((T, 2 * D_MODEL), F32),
            pltpu.VMEM((T, D_MODEL), F32),
            pltpu.VMEM((SSD_HEADS * SSD_HEAD_DIM, SSD_STATE), F32),
        ],
        compiler_params=_cparams(2),
        name="ssd",
    )(proj3, proj3, proj3, dt3, cw, cb, dtb, alog, dsk, ng, shifts)


def _rope(t, c, s):
    lane = lax.broadcasted_iota(jnp.int32, t.shape, 1)
    first_half = (lane % ATTN_HEAD_DIM) < (ROT_DIM // 2)
    partner = jnp.where(first_half, pltpu.roll(t, 128 - ROT_DIM // 2, axis=1),
                        pltpu.roll(t, ROT_DIM // 2, axis=1))
    return t * c + partner * s


def _mask_features():
    r = np.arange(ATT_W)
    kf = np.zeros((ATT_W, 128), np.float32)
    kf[r, r // CHUNK] = 1.0
    qc = np.arange(ATT_T) // CHUNK
    qf = np.zeros((3, 128, ATT_T), np.float32)
    for c in range(ATT_W // CHUNK):
        qf[TYPE_EVEN, c, :] = np.where(c > qc, -MASK_BIG, 0.0)
        qf[TYPE_ODD, c, :] = np.where(c > qc + ATT_T // CHUNK, -MASK_BIG, 0.0)
    return jnp.asarray(kf, BF16), jnp.asarray(qf, BF16)


def _attn_schedule(nb):
    rows = []
    for b in range(nb):
        for w in range(b // 2 + 1):
            rows.append((b, w, TYPE_PLAIN if w < b // 2 else TYPE_EVEN + b % 2, 0))
    return np.asarray(rows, np.int32)


def _attn_kernel(desc_ref, q_ref, k_ref, v_ref, c_ref, s_ref, lp_ref, g_ref, kf_ref, qf_ref, o_ref,
                 qz, ks, km, vt, vtm, sa, sb, pa, pb, m_all, l_all, acc_all, *, seq, n_entries):
    A = ATT_T
    W = ATT_W
    RC = ATT_RC
    n = o_ref.shape[1]
    nb = seq // A
    lo_row = lax.broadcasted_iota(jnp.int32, (128, A), 0) < ATTN_HEAD_DIM
    qscale = ATTN_HEAD_DIM ** -0.5 * LOG2E

    def load(ref, rows):
        return ref[0, rows, :].astype(F32)

    def store_q(j, qf):
        qt = qf.T
        qz[0, j] = jnp.where(lo_row, qt, 0.0).astype(BF16)
        qz[1, j] = jnp.where(lo_row, 0.0, qt).astype(BF16)

    mrows = slice(0, N_META)
    cm, sm = c_ref[mrows, :], s_ref[mrows, :]
    km[...] = _rope(load(k_ref, mrows), cm, sm).astype(BF16)
    q_meta = _rope(load(q_ref, mrows), cm, sm) * qscale
    store_q(nb, jnp.concatenate([q_meta, jnp.zeros((A - N_META, 128), F32)], axis=0))
    v_meta = jnp.concatenate([load(v_ref, mrows), jnp.zeros((128 - N_META, 128), F32)], axis=0)
    vtm[...] = v_meta.T.astype(BF16)
    o_ref[0, N_META + seq:n, :] = jnp.zeros((n - N_META - seq, 128), o_ref.dtype)
    m_all[...] = jnp.full(m_all.shape, NEG_BIG, F32)
    l_all[...] = jnp.zeros(l_all.shape, F32)
    acc_all[...] = jnp.zeros(acc_all.shape, F32)
    if nb % 2:
        ks[nb * A:(nb + 1) * A, :] = jnp.zeros((A, 128), BF16)
        vt[nb] = jnp.zeros((128, A), BF16)

    def prep_block(j):
        rows = pl.ds(pl.multiple_of(N_META + j * A, 16), A)
        c, s = c_ref[rows, :], s_ref[rows, :]
        store_q(j, _rope(load(q_ref, rows), c, s) * qscale)
        ks[pl.ds(pl.multiple_of(j * A, A), A), :] = _rope(load(k_ref, rows), c, s).astype(BF16)
        vt[j] = load(v_ref, rows).T.astype(BF16)

    def prep_group(u, carry):
        for k in range(ATT_PREP):
            prep_block(ATT_PREP * u + k)
        return carry

    lax.fori_loop(0, nb // ATT_PREP, prep_group, 0)
    for j in range(nb - nb % ATT_PREP, nb):
        prep_block(j)

    lp = lp_ref[...]
    lam_init = lp[4:5, 0:1]
    lam = (jnp.exp(jnp.sum(lp[0:1] * lp[1:2], axis=1, keepdims=True))
           - jnp.exp(jnp.sum(lp[2:3] * lp[3:4], axis=1, keepdims=True)) + lam_init)
    out_gain = g_ref[...] * (1.0 - lam_init)

    def step(t, par, mx, al):
        has_y = isinstance(t, jax.Array) or t < n_entries
        has_x = isinstance(t, jax.Array) or 1 <= t <= n_entries
        has_z = isinstance(t, jax.Array) or 2 <= t <= n_entries + 1
        s_y, s_x = (sa, sb) if par == 0 else (sb, sa)
        p_z, p_x = (pa, pb) if par == 0 else (pb, pa)
        new_mx, new_al = mx, al
        if has_z:
            qbz, wz = desc_ref[t - 2, 0], desc_ref[t - 2, 1]
            for m in range(2):
                pv = (jnp.dot(vt[2 * wz], p_z[m, 0:A, :], preferred_element_type=F32)
                      + jnp.dot(vt[2 * wz + 1], p_z[m, A:W, :], preferred_element_type=F32))
                acc_all[qbz, m] = al[m] * acc_all[qbz, m] + pv
        if has_x:
            qbx = desc_ref[t - 1, 0]
            m_old = [m_all[qbx, m] for m in range(2)]
            m_new = [jnp.maximum(m_old[m], mx[m]) for m in range(2)]
            new_al = tuple(jnp.exp2(m_old[m] - m_new[m]) for m in range(2))
            psum = [[], []]
        if has_y:
            qby, wy = desc_ref[t, 0], desc_ref[t, 1]
            feat = qf_ref[desc_ref[t, 2]]
            rhs = [jnp.concatenate([qz[m, qby], feat], axis=0) for m in range(2)]
            ymax = [[], []]
        for c in range(W // RC):
            rows = slice(c * RC, (c + 1) * RC)
            if has_y:
                krows = pl.ds(pl.multiple_of(wy * W + c * RC, 16), RC)
                kc = jnp.concatenate([ks[krows, :], kf_ref[rows, :]], axis=1)
            for m in range(2):
                if has_x:
                    p = jnp.exp2(s_x[m, rows, :] - m_new[m])
                    psum[m].append(jnp.sum(p, axis=0, keepdims=True))
                    p_x[m, rows, :] = p.astype(BF16)
                if has_y:
                    s = jnp.dot(kc, rhs[m], preferred_element_type=F32)
                    s_y[m, rows, :] = s
                    ymax[m].append(jnp.max(s, axis=0, keepdims=True))
        if has_x:
            for m in range(2):
                l_all[qbx, m] = new_al[m] * l_all[qbx, m] + functools.reduce(jnp.add, psum[m])
                m_all[qbx, m] = m_new[m]
        if has_y:
            new_mx = tuple(functools.reduce(jnp.maximum, ymax[m]) for m in range(2))
        return new_mx, new_al

    row0 = jnp.zeros((1, A), F32)
    mx, al = (row0, row0), (row0, row0)
    for t in (0, 1):
        mx, al = step(t, t % 2, mx, al)

    def group(u, carry):
        mx, al = carry[0:2], carry[2:4]
        for k in range(ATT_UNROLL):
            mx, al = step(2 + ATT_UNROLL * u + k, k % 2, mx, al)
        return mx + al

    n_mid = max(n_entries - 2, 0)
    carry = lax.fori_loop(0, n_mid // ATT_UNROLL, group, mx + al)
    mx, al = carry[0:2], carry[2:4]
    for t in range(2 + ATT_UNROLL * (n_mid // ATT_UNROLL), n_entries + 2):
        mx, al = step(t, t % 2, mx, al)

    kme = jnp.concatenate([km[...], jnp.zeros((N_META, 128), BF16)], axis=1)

    def finish(qb):
        outs = []
        for m in range(2):
            rhs = jnp.concatenate([qz[m, qb], qf_ref[TYPE_PLAIN]], axis=0)
            s = jnp.dot(kme, rhs, preferred_element_type=F32)
            m_old = m_all[qb, m]
            m_new = jnp.maximum(m_old, jnp.max(s, axis=0, keepdims=True))
            alpha = jnp.exp2(m_old - m_new)
            p = jnp.exp2(s - m_new)
            l = alpha * l_all[qb, m] + jnp.sum(p, axis=0, keepdims=True)
            pp = jnp.concatenate([p.astype(BF16), jnp.zeros((128 - N_META, A), BF16)], axis=0)
            acc = alpha * acc_all[qb, m] + jnp.dot(vtm[...], pp, preferred_element_type=F32)
            outs.append(acc * (1.0 / l))
        ot = (outs[0] - lam * outs[1]).T
        ms = jnp.mean(ot * ot, axis=-1, keepdims=True)
        res = (ot * lax.rsqrt(ms + SUBLN_EPS) * out_gain).astype(o_ref.dtype)

        if qb == nb:
            o_ref[0, 0:N_META, :] = res[0:N_META]
        else:
            o_ref[0, N_META + qb * A:N_META + (qb + 1) * A, :] = res

    for qb in range(nb + 1):
        finish(qb)


def _attention(proj3, rope_c, rope_s, lam_params, subln_g, seq):
    bsz, n, _ = proj3.shape
    A = ATT_T
    assert seq % A == 0 and n >= N_META + seq
    nb = seq // A
    hb = D_MODEL // 128
    nwin = -(-nb // 2)
    kfeat, qfeat = _mask_features()
    desc = _attn_schedule(nb)
    kern = functools.partial(_attn_kernel, seq=seq, n_entries=desc.shape[0])
    desc = jnp.asarray(desc)
    const = lambda shape: pl.BlockSpec(shape, lambda b, h: (0,) * len(shape))
    return pl.pallas_call(
        kern,
        grid=(bsz, ATTN_HEADS),
        in_specs=[
            pl.BlockSpec(memory_space=pltpu.SMEM),
            pl.BlockSpec((1, n, 128), lambda b, h: (b, 0, COL_Q * hb + h)),
            pl.BlockSpec((1, n, 128), lambda b, h: (b, 0, COL_K * hb + h)),
            pl.BlockSpec((1, n, 128), lambda b, h: (b, 0, COL_V * hb + h)),
            const((n, 128)), const((n, 128)), const((8, 128)), const((1, 128)),
            const((ATT_W, 128)), const((3, 128, A)),
        ],
        out_specs=pl.BlockSpec((1, n, 128), lambda b, h: (b, 0, h)),
        out_shape=jax.ShapeDtypeStruct((bsz, n, D_MODEL), BF16),
        scratch_shapes=[
            pltpu.VMEM((2, nb + 1, 128, A), BF16),
            pltpu.VMEM((nwin * ATT_W, 128), BF16),
            pltpu.VMEM((N_META, 128), BF16),
            pltpu.VMEM((2 * nwin, 128, A), BF16),
            pltpu.VMEM((128, 128), BF16),
            pltpu.VMEM((2, ATT_W, A), F32),
            pltpu.VMEM((2, ATT_W, A), F32),
            pltpu.VMEM((2, ATT_W, A), BF16),
            pltpu.VMEM((2, ATT_W, A), BF16),
            pltpu.VMEM((nb + 1, 2, 1, A), F32),
            pltpu.VMEM((nb + 1, 2, 1, A), F32),
            pltpu.VMEM((nb + 1, 2, 128, A), F32),
        ],
        compiler_params=_cparams(2),
        name="diff_attn",
    )(desc, proj3, proj3, proj3, rope_c, rope_s, lam_params, subln_g, kfeat, qfeat)


def _merge_kernel(h_ref, ys_ref, ya_ref, gs_ref, ga_ref, ws_ref, wa_ref, wo_ref, o_ref):
    a = jnp.dot(ys_ref[...], ws_ref[...], preferred_element_type=F32)
    b = jnp.dot(ya_ref[...], wa_ref[...], preferred_element_type=F32)
    m = _sigmoid(gs_ref[...].astype(F32)) * a + _sigmoid(ga_ref[...].astype(F32)) * b
    o_ref[...] = h_ref[...] + jnp.dot(m.astype(BF16), wo_ref[...], preferred_element_type=F32)


def _merge(h2d, ys2d, ya2d, proj2d, ws, wa, wo):
    m = h2d.shape[0]
    tm = _pick(m, (768, 512, 256, 128))
    row = lambda c: pl.BlockSpec((tm, D_MODEL), lambda i: (i, c))
    wspec = pl.BlockSpec((D_MODEL, D_MODEL), lambda i: (0, 0))
    return pl.pallas_call(
        _merge_kernel,
        grid=(m // tm,),
        in_specs=[row(0), row(0), row(0), row(COL_GS), row(COL_GA), wspec, wspec, wspec],
        out_specs=row(0),
        out_shape=jax.ShapeDtypeStruct((m, D_MODEL), F32),
        compiler_params=_cparams(1),
        name="merge",
    )(h2d, ys2d, ya2d, proj2d, proj2d, ws, wa, wo)


def _mlp_kernel(h_ref, g_ref, wg_ref, wv_ref, cwg_ref, cwv_ref, cbg_ref, cbv_ref, wd_ref, o_ref,
                u_ref, hbuf, carry):
    i = pl.program_id(1)
    j = pl.program_id(2)
    tm = u_ref.shape[0]

    @pl.when(j == 0)
    def _():
        x = h_ref[0]
        ms = jnp.mean(x * x, axis=-1, keepdims=True)
        u_ref[...] = (x * lax.rsqrt(ms + NORM_EPS) * g_ref[...]).astype(BF16)
        o_ref[0] = x

    u = u_ref[...]
    convs = []
    for idx, (w_ref, cw_ref, cb_ref) in enumerate(((wg_ref, cwg_ref, cbg_ref), (wv_ref, cwv_ref, cbv_ref))):
        hid = jnp.dot(u, w_ref[...], preferred_element_type=F32)
        prev = carry[j, idx]
        hbuf[idx, 0:8, :] = jnp.where(i == 0, jnp.zeros_like(prev), prev)
        hbuf[idx, 8:8 + tm, :] = hid
        conv = cb_ref[...] + cw_ref[2:3, :] * hbuf[idx, 8:8 + tm, :]
        conv = conv + cw_ref[1:2, :] * hbuf[idx, 7:7 + tm, :]
        conv = conv + cw_ref[0:1, :] * hbuf[idx, 6:6 + tm, :]
        carry[j, idx] = hbuf[idx, tm:tm + 8, :]
        convs.append(conv)
    act = (_silu(convs[0]) * convs[1]).astype(BF16)
    o_ref[0] += jnp.dot(act, wd_ref[...], preferred_element_type=F32)


def _mlp(h3, g, w_up, cw, cb, w_down):
    bsz, n, _ = h3.shape
    tm = _pick(n, (2112, 1056, 1024, 768, 512, 384, 256, 128))
    nj = D_FF // FF_T
    return pl.pallas_call(
        _mlp_kernel,
        grid=(bsz, n // tm, nj),
        in_specs=[
            pl.BlockSpec((1, tm, D_MODEL), lambda b, i, j: (b, i, 0)),
            pl.BlockSpec((1, D_MODEL), lambda b, i, j: (0, 0)),
            pl.BlockSpec((D_MODEL, FF_T), lambda b, i, j: (0, j)),
            pl.BlockSpec((D_MODEL, FF_T), lambda b, i, j: (0, j + nj)),
            pl.BlockSpec((MLP_CONV, FF_T), lambda b, i, j: (0, j)),
            pl.BlockSpec((MLP_CONV, FF_T), lambda b, i, j: (0, j + nj)),
            pl.BlockSpec((1, FF_T), lambda b, i, j: (0, j)),
            pl.BlockSpec((1, FF_T), lambda b, i, j: (0, j + nj)),
            pl.BlockSpec((FF_T, D_MODEL), lambda b, i, j: (j, 0)),
        ],
        out_specs=pl.BlockSpec((1, tm, D_MODEL), lambda b, i, j: (b, i, 0)),
        out_shape=jax.ShapeDtypeStruct((bsz, n, D_MODEL), F32),
        scratch_shapes=[
            pltpu.VMEM((tm, D_MODEL), BF16),
            pltpu.VMEM((2, tm + 8, FF_T), F32),
            pltpu.VMEM((nj, 2, 8, FF_T), F32),
        ],
        compiler_params=_cparams(3),
        name="convglu_mlp",
    )(h3, g, w_up, w_up, cw, cw, cb, cb, w_down)


def _final_kernel(a_ref, b_ref, g_ref, o_ref):
    tm = o_ref.shape[1]

    def norm(x):
        ms = jnp.mean(x * x, axis=-1, keepdims=True)
        return x * lax.rsqrt(ms + NORM_EPS) * g_ref[...]

    o_ref[0, 0:tm - N_META, :] = norm(a_ref[0, N_META:tm, :])
    o_ref[0, tm - N_META:tm, :] = norm(b_ref[0])


def _final_norm(h3, g, seq):
    bsz = h3.shape[0]
    tm = _pick(seq, (512, 256, 128, 64, 32))
    per = tm // N_META
    return pl.pallas_call(
        _final_kernel,
        grid=(bsz, seq // tm),
        in_specs=[
            pl.BlockSpec((1, tm, D_MODEL), lambda b, i: (b, i, 0)),
            pl.BlockSpec((1, N_META, D_MODEL), lambda b, i: (b, (i + 1) * per, 0)),
            pl.BlockSpec((1, D_MODEL), lambda b, i: (0, 0)),
        ],
        out_specs=pl.BlockSpec((1, tm, D_MODEL), lambda b, i: (b, i, 0)),
        out_shape=jax.ShapeDtypeStruct((bsz, seq, D_MODEL), F32),
        compiler_params=_cparams(2),
        name="final_norm",
    )(h3, h3, g)


def _rope_tables(n_pos):
    half = ROT_DIM // 2
    inv = 1.0 / (ROPE_THETA ** (jnp.arange(half, dtype=F32) * 2.0 / ROT_DIM))
    ang = jnp.arange(n_pos, dtype=F32)[:, None] * inv[None, :]
    cos, sin = jnp.cos(ang), jnp.sin(ang)
    rest = ATTN_HEAD_DIM - ROT_DIM
    c64 = jnp.concatenate([cos, cos, jnp.ones((n_pos, rest), F32)], axis=1)
    s64 = jnp.concatenate([-sin, sin, jnp.zeros((n_pos, rest), F32)], axis=1)
    return jnp.concatenate([c64, c64], axis=1), jnp.concatenate([s64, s64], axis=1)


def _pad_lanes(v, width=128):
    v = v.astype(F32).reshape(1, -1)
    return jnp.pad(v, ((0, 0), (0, width - v.shape[1])))


def kernel(x, meta_tokens, norm1_g, w_in, ssd_conv_w, ssd_conv_b, ssd_dt_bias, ssd_a_log, ssd_d, ssd_norm_g, lambda_q1, lambda_k1, lambda_q2, lambda_k2, attn_subln_g, w_ssd_branch, w_attn_branch, w_out, norm2_g, w_up, mlp_conv_w, mlp_conv_b, w_down, final_norm_g):
    bsz, seq, _ = x.shape
    depth = w_in.shape[0]
    n_tok = N_META + seq
    n = -(-n_tok // Q_BLOCK) * Q_BLOCK
    meta = jnp.broadcast_to(meta_tokens.astype(x.dtype)[None], (bsz, N_META, D_MODEL))
    h = jnp.concatenate([meta, x, jnp.zeros((bsz, n - n_tok, D_MODEL), x.dtype)], axis=1)
    rope_c, rope_s = _rope_tables(n)
    dt_lo = 3 * D_MODEL
    dt_hi = dt_lo + SSD_HEADS

    for l in range(depth):
        lam_init = 0.8 - 0.6 * math.exp(-0.3 * l)
        w_main = jnp.concatenate([w_in[l][:, :dt_lo], w_in[l][:, dt_hi:]], axis=1).astype(BF16)
        w_dt = jnp.pad(w_in[l][:, dt_lo:dt_hi], ((0, 0), (0, 128 - SSD_HEADS))).astype(BF16)
        proj, dt_raw = _inproj(h.reshape(bsz * n, D_MODEL), norm1_g[l].reshape(1, D_MODEL), w_main, w_dt)
        proj3 = proj.reshape(bsz, n, N_PROJ)

        y_ssd = _ssd(proj3, dt_raw.reshape(bsz, n, 128), ssd_conv_w[l], ssd_conv_b[l].reshape(1, -1),
                     _pad_lanes(ssd_dt_bias[l]),
                     jnp.broadcast_to(ssd_a_log[l].astype(F32)[:, None], (SSD_HEADS, SSD_T)),
                     jnp.repeat(ssd_d[l], SSD_HEAD_DIM).reshape(1, D_MODEL),
                     ssd_norm_g[l].reshape(1, D_MODEL))

        lam_params = jnp.concatenate([
            _pad_lanes(lambda_q1[l]), _pad_lanes(lambda_k1[l]), _pad_lanes(lambda_q2[l]),
            _pad_lanes(lambda_k2[l]), jnp.full((1, 128), lam_init, F32), jnp.zeros((3, 128), F32)], axis=0)
        y_attn = _attention(proj3, rope_c, rope_s, lam_params, attn_subln_g[l].reshape(1, 128), seq)

        h2d = _merge(h.reshape(bsz * n, D_MODEL), y_ssd.reshape(bsz * n, D_MODEL),
                     y_attn.reshape(bsz * n, D_MODEL), proj,
                     w_ssd_branch[l].astype(BF16), w_attn_branch[l].astype(BF16), w_out[l].astype(BF16))

        h = _mlp(h2d.reshape(bsz, n, D_MODEL), norm2_g[l].reshape(1, D_MODEL), w_up[l].astype(BF16),
                 mlp_conv_w[l], mlp_conv_b[l].reshape(1, -1), w_down[l].astype(BF16))

    return _final_norm(h, final_norm_g.reshape(1, D_MODEL), seq)
```
